```python
import math
import jax, jax.numpy as jnp
from jax import lax
import numpy as np

D_MODEL = 1024
BATCH = 8
SEQ = 8192
DEPTH = 4

N_MIXERS = 4
D_FF = 2816
EPS = 1e-6
MASK_VALUE = -1e30

DIL_PATTERNS = ((128, 1), (512, 4), (2048, 16))
N_GROUPS_A = len(DIL_PATTERNS)
HEADS_PER_GROUP = 8
HEAD_DIM_A = 64
ATTN_WIDTH = HEADS_PER_GROUP * HEAD_DIM_A
BAND_BLOCK = 64

CONV_WIDTH = 31

FOURIER_GROUPS = 4
FOURIER_GROUP_DIM = D_MODEL // FOURIER_GROUPS

GLA_HEADS = 4
GLA_DK = D_MODEL // 2
GLA_DV = D_MODEL
GLA_HEAD_K = GLA_DK // GLA_HEADS
GLA_HEAD_V = GLA_DV // GLA_HEADS
GLA_GATE_RANK = 16
GLA_TAU = 16.0
GLA_CHUNK = 64

N_ATTN_LAYERS = (DEPTH + 3) // 4
N_CONV_LAYERS = (DEPTH + 2) // 4
N_FOURIER_LAYERS = (DEPTH + 1) // 4
N_GLA_LAYERS = DEPTH // 4

kernel_name = "hybrid_interleaved_bidir_encoder"

F32 = jnp.float32


def _rmsnorm(x, g):
    xf = x.astype(F32)
    y = xf * lax.rsqrt(jnp.mean(xf * xf, axis=-1, keepdims=True) + EPS)
    return (y * g.astype(F32)).astype(x.dtype)


def _swiglu(h, w1, w3, w2):
    return (jax.nn.silu(h @ w1) * (h @ w3)) @ w2


def _alibi_slopes():
    n = N_GROUPS_A * HEADS_PER_GROUP
    return jnp.exp2(-8.0 * jnp.arange(1, n + 1, dtype=F32) / n)


def _band_attention(q, k, v, radius, slopes_dist):
    n, L, h, dh = q.shape
    nb = -(-L // BAND_BLOCK)
    Lp = nb * BAND_BLOCK
    pad_end = Lp - L
    qp = jnp.pad(q, ((0, 0), (0, pad_end), (0, 0), (0, 0))).reshape(n, nb, BAND_BLOCK, h, dh)

    def windows(t):
        tp = jnp.pad(t, ((0, 0), (BAND_BLOCK, BAND_BLOCK + pad_end), (0, 0), (0, 0)))
        tp = tp.reshape(n, nb + 2, BAND_BLOCK, h, dh)
        return jnp.concatenate([tp[:, :-2], tp[:, 1:-1], tp[:, 2:]], axis=2)

    kw, vw = windows(k), windows(v)
    qi = jnp.arange(Lp).reshape(nb, BAND_BLOCK)
    kj = (jnp.arange(nb)[:, None] - 1) * BAND_BLOCK + jnp.arange(3 * BAND_BLOCK)[None, :]
    rel = kj[:, None, :] - qi[:, :, None]
    valid = (jnp.abs(rel) <= radius) & (kj[:, None, :] >= 0) & (kj[:, None, :] < L)
    s = jnp.einsum('nbqhd,nbkhd->nhbqk', qp.astype(F32), kw.astype(F32)) * (dh ** -0.5)
    s = s - slopes_dist[:, None, None, None] * jnp.abs(rel).astype(F32)
    s = jnp.where(valid, s, MASK_VALUE)
    lse = jax.nn.logsumexp(s, axis=-1)
    p = jnp.exp(s - lse[..., None])
    o = jnp.einsum('nhbqk,nbkhd->nbqhd', p, vw.astype(F32)).reshape(n, Lp, h, dh)[:, :L]
    lse = lse.transpose(0, 2, 3, 1).reshape(n, Lp, h)[:, :L]
    return o, lse


def _dilated_group(q, k, v, window, dilation, slopes):
    b, s, h, dh = q.shape
    sub = s // dilation

    def split(t):
        return t.reshape(b, sub, dilation, h, dh).transpose(0, 2, 1, 3, 4).reshape(b * dilation, sub, h, dh)

    o, lse = _band_attention(split(q), split(k), split(v), window // (2 * dilation), slopes * dilation)
    o = o.reshape(b, dilation, sub, h, dh).transpose(0, 2, 1, 3, 4).reshape(b, s, h, dh)
    lse = lse.reshape(b, dilation, sub, h).transpose(0, 2, 1, 3).reshape(b, s, h)
    return o, lse


def _dilated_attention_mixer(u, w_qkv, w_o):
    b, s, _ = u.shape
    qkv = (u @ w_qkv).reshape(b, s, N_GROUPS_A, 3, HEADS_PER_GROUP, HEAD_DIM_A)
    slopes = _alibi_slopes()
    outs, lses = [], []
    for g, (win, dil) in enumerate(DIL_PATTERNS):
        o, l = _dilated_group(qkv[:, :, g, 0], qkv[:, :, g, 1], qkv[:, :, g, 2], win, dil,
                              slopes[g * HEADS_PER_GROUP:(g + 1) * HEADS_PER_GROUP])
        outs.append(o)
        lses.append(l)
    wts = jax.nn.softmax(jnp.stack(lses, 0), axis=0)
    o = jnp.einsum('gbsh,gbshd->bshd', wts, jnp.stack(outs, 0))
    return o.reshape(b, s, ATTN_WIDTH).astype(u.dtype) @ w_o


def _conv_module(u, w_pw1, b_pw1, w_dw, b_dw, ln_g, ln_b, w_pw2, b_pw2):
    z = u @ w_pw1 + b_pw1
    a, gate = jnp.split(z, 2, axis=-1)
    z = a * jax.nn.sigmoid(gate)
    z = lax.conv_general_dilated(z, w_dw[:, None, :].astype(z.dtype), window_strides=(1,),
                                 padding=[(CONV_WIDTH // 2, CONV_WIDTH // 2)],
                                 dimension_numbers=('NWC', 'WIO', 'NWC'),
                                 feature_group_count=D_MODEL) + b_dw
    zf = z.astype(F32)
    mu = jnp.mean(zf, axis=-1, keepdims=True)
    var = jnp.mean(jnp.square(zf - mu), axis=-1, keepdims=True)
    zf = (zf - mu) * lax.rsqrt(var + EPS) * ln_g.astype(F32) + ln_b.astype(F32)
    return jax.nn.silu(zf).astype(u.dtype) @ w_pw2 + b_pw2


def _fourier_mixer(u, w_f, b_f):
    b, s, _ = u.shape
    uf = u.astype(F32).reshape(b, s, FOURIER_GROUPS, FOURIER_GROUP_DIM)
    mixed = jnp.fft.fft2(uf, axes=(1, 3), norm='ortho').real
    return mixed.reshape(b, s, D_MODEL).astype(u.dtype) @ w_f + b_f


def _gla_direction(q, k, v, log_a, include_diag):
    b, s, H, dk = q.shape
    dv = v.shape[-1]
    n = s // GLA_CHUNK

    def chunk(t):
        return t.reshape(b, n, GLA_CHUNK, H, t.shape[-1]).astype(F32)

    qc, kc, vc, gc = chunk(q), chunk(k), chunk(v), chunk(log_a)
    cum = jnp.cumsum(gc, axis=2)
    ref = cum[:, :, GLA_CHUNK // 2 - 1:GLA_CHUNK // 2]
    scores = jnp.einsum('bnihk,bnjhk->bnhij', qc * jnp.exp(cum - ref), kc * jnp.exp(ref - cum))
    mask = jnp.tril(jnp.ones((GLA_CHUNK, GLA_CHUNK), dtype=bool), k=0 if include_diag else -1)
    o_intra = jnp.einsum('bnhij,bnjhv->bnihv', jnp.where(mask, scores, 0.0), vc)
    last = cum[:, :, -1]
    q_inter = qc * jnp.exp(cum)
    k_state = kc * jnp.exp(last[:, :, None] - cum)

    def step(state, inp):
        q_i, k_s, v_c, dec = inp
        o = jnp.einsum('bihk,bhkv->bihv', q_i, state)
        state = state * jnp.exp(dec)[..., None] + jnp.einsum('bjhk,bjhv->bhkv', k_s, v_c)
        return state, o

    state0 = jnp.zeros((b, H, dk, dv), F32)
    xs = (jnp.moveaxis(q_inter, 1, 0), jnp.moveaxis(k_state, 1, 0),
          jnp.moveaxis(vc, 1, 0), jnp.moveaxis(last, 1, 0))
    _, o_inter = lax.scan(step, state0, xs)
    o = o_intra + jnp.moveaxis(o_inter, 0, 1)
    return o.reshape(b, s, H, dv)


def _gla_mixer(u, w_in, w_a1, w_a2, b_a, norm_g, w_o):
    b, s, _ = u.shape
    proj = u @ w_in
    q, k, v, r = jnp.split(proj, [GLA_DK, 2 * GLA_DK, 2 * GLA_DK + GLA_DV], axis=-1)
    q = q.reshape(b, s, GLA_HEADS, GLA_HEAD_K) * (GLA_HEAD_K ** -0.5)
    k = k.reshape(b, s, GLA_HEADS, GLA_HEAD_K)
    v = v.reshape(b, s, GLA_HEADS, GLA_HEAD_V)

    def log_decay(d):
        z = ((u @ w_a1[d]) @ w_a2[d] + b_a[d]).astype(F32)
        return (jax.nn.log_sigmoid(z) / GLA_TAU).reshape(b, s, GLA_HEADS, GLA_HEAD_K)

    o_fwd = _gla_direction(q, k, v, log_decay(0), True)
    o_bwd = _gla_direction(q[:, ::-1], k[:, ::-1], v[:, ::-1], log_decay(1)[:, ::-1], False)[:, ::-1]
    o = o_fwd + o_bwd
    o = o * lax.rsqrt(jnp.mean(o * o, axis=-1, keepdims=True) + EPS)
    o = o.reshape(b, s, GLA_DV) * norm_g.astype(F32) * jax.nn.silu(r.astype(F32))
    return o.astype(u.dtype) @ w_o


def setup_inputs(seed: int = 0) -> dict:
    key = jax.random.key(seed)
    ks = iter(jax.random.split(key, 32))

    def nrm(shape, fan_in):
        return jax.random.normal(next(ks), shape, F32) * (fan_in ** -0.5)

    def gain(shape):
        return 1.0 + 0.02 * jax.random.normal(next(ks), shape, F32)

    def bias(shape, scale=0.02):
        return scale * jax.random.normal(next(ks), shape, F32)

    D = D_MODEL
    return {
        "x": jax.random.normal(next(ks), (BATCH, SEQ, D), F32),
        "norm_g": gain((DEPTH, 3, D)),
        "final_norm_g": gain((D,)),
        "ffn_w1": nrm((DEPTH, 2, D, D_FF), D),
        "ffn_w3": nrm((DEPTH, 2, D, D_FF), D),
        "ffn_w2": nrm((DEPTH, 2, D_FF, D), D_FF),
        "attn_w_qkv": nrm((N_ATTN_LAYERS, D, N_GROUPS_A * 3 * ATTN_WIDTH), D),
        "attn_w_o": nrm((N_ATTN_LAYERS, ATTN_WIDTH, D), ATTN_WIDTH),
        "conv_w_pw1": nrm((N_CONV_LAYERS, D, 2 * D), D),
        "conv_b_pw1": bias((N_CONV_LAYERS, 2 * D)),
        "conv_w_dw": nrm((N_CONV_LAYERS, CONV_WIDTH, D), CONV_WIDTH),
        "conv_b_dw": bias((N_CONV_LAYERS, D)),
        "conv_ln_g": gain((N_CONV_LAYERS, D)),
        "conv_ln_b": bias((N_CONV_LAYERS, D)),
        "conv_w_pw2": nrm((N_CONV_LAYERS, D, D), D),
        "conv_b_pw2": bias((N_CONV_LAYERS, D)),
        "fnet_w": nrm((N_FOURIER_LAYERS, D, D), D),
        "fnet_b": bias((N_FOURIER_LAYERS, D)),
        "gla_w_in": nrm((N_GLA_LAYERS, D, 2 * GLA_DK + 2 * GLA_DV), D),
        "gla_w_a1": nrm((N_GLA_LAYERS, 2, D, GLA_GATE_RANK), D),
        "gla_w_a2": nrm((N_GLA_LAYERS, 2, GLA_GATE_RANK, GLA_DK), GLA_GATE_RANK),
        "gla_b_a": bias((N_GLA_LAYERS, 2, GLA_DK), 0.1),
        "gla_norm_g": gain((N_GLA_LAYERS, GLA_DV)),
        "gla_w_o": nrm((N_GLA_LAYERS, GLA_DV, D), GLA_DV),
    }


def reference(x, norm_g, final_norm_g, ffn_w1, ffn_w3, ffn_w2, attn_w_qkv, attn_w_o,
              conv_w_pw1, conv_b_pw1, conv_w_dw, conv_b_dw, conv_ln_g, conv_ln_b,
              conv_w_pw2, conv_b_pw2, fnet_w, fnet_b, gla_w_in, gla_w_a1, gla_w_a2,
              gla_b_a, gla_norm_g, gla_w_o):
    h = x
    for i in range(DEPTH):
        m, j = i % N_MIXERS, i // N_MIXERS
        h = h + 0.5 * _swiglu(_rmsnorm(h, norm_g[i, 0]), ffn_w1[i, 0], ffn_w3[i, 0], ffn_w2[i, 0])
        u = _rmsnorm(h, norm_g[i, 1])
        if m == 0:
            mix = _dilated_attention_mixer(u, attn_w_qkv[j], attn_w_o[j])
        elif m == 1:
            mix = _conv_module(u, conv_w_pw1[j], conv_b_pw1[j], conv_w_dw[j], conv_b_dw[j],
                               conv_ln_g[j], conv_ln_b[j], conv_w_pw2[j], conv_b_pw2[j])
        elif m == 2:
            mix = _fourier_mixer(u, fnet_w[j], fnet_b[j])
        else:
            mix = _gla_mixer(u, gla_w_in[j], gla_w_a1[j], gla_w_a2[j], gla_b_a[j],
                             gla_norm_g[j], gla_w_o[j])
        h = h + mix
        h = h + 0.5 * _swiglu(_rmsnorm(h, norm_g[i, 2]), ffn_w1[i, 1], ffn_w3[i, 1], ffn_w2[i, 1])
    return _rmsnorm(h, final_norm_g)
```

```python
import functools
import math

import numpy as np
import jax
import jax.numpy as jnp
from jax import lax
from jax.experimental import pallas as pl
from jax.experimental.pallas import tpu as pltpu

F32 = jnp.float32
BF16 = jnp.bfloat16
EPS = 1e-6
MASK_VALUE = -1e30

LANES = 128
VMEM_LIMIT_BYTES = 56 * 1024 * 1024

DIL_PATTERNS = ((128, 1), (512, 4), (2048, 16))
HEADS_PER_GROUP = 8
HEAD_DIM_A = 64
ATTN_WIDTH = HEADS_PER_GROUP * HEAD_DIM_A
BAND_RADIUS = 64
CONV_WIDTH = 31
CONV_HALO = 16
FOURIER_GROUP_DIM = 256
GLA_HEADS = 4
GLA_HEAD_K = 128
GLA_HEAD_V = 256
GLA_DK = GLA_HEADS * GLA_HEAD_K
GLA_DV = GLA_HEADS * GLA_HEAD_V
GLA_TAU = 16.0
GLA_CHUNK = 64

NT_DIMS = (((1,), (1,)), ((), ()))
TN_DIMS = (((0,), (0,)), ((), ()))


def _params(*semantics):
    return pltpu.CompilerParams(dimension_semantics=semantics, vmem_limit_bytes=VMEM_LIMIT_BYTES)


def _resident(shape):
    nd = len(shape)
    return pl.BlockSpec(shape, lambda *_: (0,) * nd, pipeline_mode=pl.Buffered(1))


def _rms(x, g):
    ms = jnp.mean(x * x, axis=-1, keepdims=True)
    return x * lax.rsqrt(ms + EPS) * g


def _silu(x):
    return x * jax.nn.sigmoid(x)


def _dot(a, b):
    return jnp.dot(a, b, preferred_element_type=F32)


def _ffn_body(x_ref, g_ref, w1_ref, w3_ref, w2_ref, gf_ref, o_ref, gate_ref, *, ff_chunk, final):
    x = x_ref[...]
    hn = _rms(x, g_ref[...]).astype(BF16)
    d_ff = w1_ref.shape[1]
    for c0 in range(0, d_ff, ff_chunk):
        sl = slice(c0, c0 + ff_chunk)
        a = _dot(hn, w1_ref[:, sl])
        b = _dot(hn, w3_ref[:, sl])
        gate_ref[:, sl] = (_silu(a) * b).astype(BF16)
    y = x + 0.5 * _dot(gate_ref[...], w2_ref[...])
    if final:
        y = _rms(y, gf_ref[...])
    o_ref[...] = y


def _ffn(h2, g, w1, w3, w2, gf, *, final, tm=512, ff_chunk=256):
    t, d = h2.shape
    d_ff = w1.shape[1]
    row = pl.BlockSpec((tm, d), lambda i: (i, 0))
    return pl.pallas_call(
        functools.partial(_ffn_body, ff_chunk=ff_chunk, final=final),
        grid=(t // tm,),
        in_specs=[row, _resident((1, d)), _resident((d, d_ff)), _resident((d, d_ff)),
                  _resident((d_ff, d)), _resident((1, d))],
        out_specs=row,
        out_shape=jax.ShapeDtypeStruct((t, d), F32),
        scratch_shapes=[pltpu.VMEM((tm, d_ff), BF16)],
        compiler_params=_params("parallel"),
        name="ffn_final" if final else "ffn",
    )(h2, g, w1, w3, w2, gf)


def _attn_proj_body(x_ref, g_ref, w_ref, o_ref):
    u = _rms(x_ref[...], g_ref[...]).astype(BF16)
    o_ref[...] = _dot(u, w_ref[...]).astype(BF16)


def _attn_proj(h, g, w, dil, tm=512):
    b, s, d = h.shape
    sub = s // dil
    tm = min(tm, sub)
    n = w.shape[1]
    return pl.pallas_call(
        _attn_proj_body,
        grid=(b, dil, sub // tm),
        in_specs=[pl.BlockSpec((None, tm, d), lambda bi, r, i: (bi, i, r)),
                  _resident((1, d)), _resident((d, n))],
        out_specs=pl.BlockSpec((None, None, tm, n), lambda bi, r, i: (bi, r, i, 0)),
        out_shape=jax.ShapeDtypeStruct((b, dil, sub, n), BF16),
        compiler_params=_params("parallel", "parallel", "parallel"),
        name=f"attn_proj_d{dil}",
    )(h.reshape(b, sub, dil * d), g, w)


def _band_body(q_ref, kp_ref, km_ref, kn_ref, vp_ref, vm_ref, vn_ref, o_ref, lse_ref, *,
               tq, sub, slopes):
    i = pl.program_id(2)
    nk = tq + 2 * BAND_RADIUS
    r = lax.broadcasted_iota(jnp.int32, (tq, nk), 0)
    c = lax.broadcasted_iota(jnp.int32, (tq, nk), 1)
    rel = c - BAND_RADIUS - r
    kpos = i * tq - BAND_RADIUS + c
    valid = (jnp.abs(rel) <= BAND_RADIUS) & (kpos >= 0) & (kpos < sub)
    absrel = jnp.abs(rel).astype(F32)
    lane = lax.broadcasted_iota(jnp.int32, (tq, LANES), 1)
    low = lane < HEAD_DIM_A
    lowf = low.astype(F32)
    for p in range(HEADS_PER_GROUP // 2):
        sl = slice(LANES * p, LANES * (p + 1))
        q2 = q_ref[:, sl].astype(F32)
        kc = jnp.concatenate([kp_ref[:, sl], km_ref[:, sl], kn_ref[:, sl]], axis=0)
        vc = jnp.concatenate([vp_ref[:, sl], vm_ref[:, sl], vn_ref[:, sl]], axis=0)
        outs, lses = [], []
        for hh in range(2):
            qm = (q2 * (lowf if hh == 0 else 1.0 - lowf)).astype(BF16)
            sc = lax.dot_general(qm, kc, NT_DIMS, preferred_element_type=F32)
            sc = jnp.where(valid, sc - slopes[2 * p + hh] * absrel, MASK_VALUE)
            m = jnp.max(sc, axis=-1, keepdims=True)
            e = jnp.exp(sc - m)
            l = jnp.sum(e, axis=-1, keepdims=True)
            outs.append(_dot(e.astype(BF16), vc) / l)
            lses.append(m + jnp.log(l))
        o_ref[:, sl] = jnp.where(low, outs[0], outs[1]).astype(BF16)
        lse_ref[:, sl] = jnp.where(low, lses[0], lses[1])


def _band_attention(qkv, group, dil, tq=256):
    b, _, sub, _ = qkv.shape
    w = ATTN_WIDTH
    tq = min(tq, sub)
    hb = tq // BAND_RADIUS
    n_halo = sub // BAND_RADIUS
    n_heads = len(DIL_PATTERNS) * HEADS_PER_GROUP
    slopes = tuple(float(2.0 ** (-8.0 * (group * HEADS_PER_GROUP + hd + 1) / n_heads)) * dil
                   for hd in range(HEADS_PER_GROUP))

    def main(col):
        return pl.BlockSpec((None, None, tq, w), lambda bi, r, i: (bi, r, i, col))

    def prev(col):
        return pl.BlockSpec((None, None, BAND_RADIUS, w),
                            lambda bi, r, i: (bi, r, jnp.maximum(i * hb - 1, 0), col))

    def nxt(col):
        return pl.BlockSpec((None, None, BAND_RADIUS, w),
                            lambda bi, r, i: (bi, r, jnp.minimum((i + 1) * hb, n_halo - 1), col))

    out_spec = pl.BlockSpec((None, tq, w), lambda bi, r, i: (bi, i, r))
    o, lse = pl.pallas_call(
        functools.partial(_band_body, tq=tq, sub=sub, slopes=slopes),
        grid=(b, dil, sub // tq),
        in_specs=[main(0), prev(1), main(1), nxt(1), prev(2), main(2), nxt(2)],
        out_specs=[out_spec, out_spec],
        out_shape=[jax.ShapeDtypeStruct((b, sub, dil * w), BF16),
                   jax.ShapeDtypeStruct((b, sub, dil * w), F32)],
        compiler_params=_params("parallel", "parallel", "parallel"),
        name=f"band_attn_d{dil}",
    )(qkv, qkv, qkv, qkv, qkv, qkv, qkv)
    return o.reshape(b, sub * dil, w), lse.reshape(b, sub * dil, w)


def _attn_out_body(o0, o1, o2, l0, l1, l2, h_ref, w_ref, out_ref):
    ls = [l0[...], l1[...], l2[...]]
    m = jnp.maximum(jnp.maximum(ls[0], ls[1]), ls[2])
    es = [jnp.exp(l - m) for l in ls]
    num = es[0] * o0[...].astype(F32) + es[1] * o1[...].astype(F32) + es[2] * o2[...].astype(F32)
    o = (num / (es[0] + es[1] + es[2])).astype(BF16)
    out_ref[...] = h_ref[...] + _dot(o, w_ref[...])


def _attn_out(os_, lses, h2, w_o, tm=512):
    t, d = h2.shape
    w = ATTN_WIDTH
    half = pl.BlockSpec((tm, w), lambda i: (i, 0))
    row = pl.BlockSpec((tm, d), lambda i: (i, 0))
    return pl.pallas_call(
        _attn_out_body,
        grid=(t // tm,),
        in_specs=[half] * 6 + [row, _resident((w, d))],
        out_specs=row,
        out_shape=jax.ShapeDtypeStruct((t, d), F32),
        compiler_params=_params("parallel"),
        name="attn_out",
    )(*[o.reshape(t, w) for o in os_], *[l.reshape(t, w) for l in lses], h2, w_o)


def _attention_mixer(h, g, w_qkv, w_o):
    b, s, d = h.shape
    w = ATTN_WIDTH
    os_, lses = [], []
    for grp, (_, dil) in enumerate(DIL_PATTERNS):
        wg = w_qkv[:, grp * 3 * w:(grp + 1) * 3 * w]
        wg = jnp.concatenate([wg[:, :w] * (HEAD_DIM_A ** -0.5), wg[:, w:]], axis=1).astype(BF16)
        qkv = _attn_proj(h, g, wg, dil)
        o, lse = _band_attention(qkv, grp, dil)
        os_.append(o)
        lses.append(lse)
    return _attn_out(os_, lses, h.reshape(b * s, d), w_o.astype(BF16)).reshape(b, s, d)


def _conv_glu_body(x_ref, g_ref, w_ref, b_ref, o_ref):
    u = _rms(x_ref[...], g_ref[...]).astype(BF16)
    z = _dot(u, w_ref[...]) + b_ref[...]
    d = o_ref.shape[1]
    o_ref[...] = (z[:, :d] * jax.nn.sigmoid(z[:, d:])).astype(BF16)


def _conv_glu(h2, g, w, bias, tm=512):
    t, d = h2.shape
    row = pl.BlockSpec((tm, d), lambda i: (i, 0))
    return pl.pallas_call(
        _conv_glu_body,
        grid=(t // tm,),
        in_specs=[row, _resident((1, d)), _resident((d, 2 * d)), _resident((1, 2 * d))],
        out_specs=row,
        out_shape=jax.ShapeDtypeStruct((t, d), BF16),
        compiler_params=_params("parallel"),
        name="conv_glu",
    )(h2, g, w, bias)


def _conv_dw_body(zp_ref, zm_ref, zn_ref, h_ref, wdw_ref, bdw_ref, lng_ref, lnb_ref, w2_ref, b2_ref,
                  o_ref, zext_ref, cv_ref, *, tm, n_tiles, row_chunk, lane_chunk):
    i = pl.program_id(1)
    d = zm_ref.shape[1]
    zext_ref[0:CONV_HALO, :] = jnp.where(i > 0, zp_ref[...].astype(F32), 0.0)
    zext_ref[CONV_HALO:CONV_HALO + tm, :] = zm_ref[...].astype(F32)
    zext_ref[CONV_HALO + tm:2 * CONV_HALO + tm, :] = jnp.where(i < n_tiles - 1, zn_ref[...].astype(F32), 0.0)
    first = CONV_HALO - CONV_WIDTH // 2
    for c0 in range(0, d, lane_chunk):
        cs = slice(c0, c0 + lane_chunk)
        for r0 in range(0, tm, row_chunk):
            acc = jnp.zeros((row_chunk, lane_chunk), F32)
            for k in range(CONV_WIDTH):
                acc = acc + wdw_ref[k:k + 1, cs] * zext_ref[r0 + first + k:r0 + first + k + row_chunk, cs]
            cv_ref[r0:r0 + row_chunk, cs] = acc + bdw_ref[:, cs]
    z = cv_ref[...]
    mu = jnp.mean(z, axis=-1, keepdims=True)
    zc = z - mu
    var = jnp.mean(zc * zc, axis=-1, keepdims=True)
    zn = zc * lax.rsqrt(var + EPS) * lng_ref[...] + lnb_ref[...]
    y = _silu(zn).astype(BF16)
    o_ref[...] = h_ref[...] + _dot(y, w2_ref[...]) + b2_ref[...]


def _conv_dw(z, h, w_dw, b_dw, ln_g, ln_b, w2, b2, tm=256):
    b, s, d = h.shape
    n_tiles = s // tm
    hb = tm // CONV_HALO
    n_halo = s // CONV_HALO
    main = pl.BlockSpec((None, tm, d), lambda bi, i: (bi, i, 0))
    prev = pl.BlockSpec((None, CONV_HALO, d), lambda bi, i: (bi, jnp.maximum(i * hb - 1, 0), 0))
    nxt = pl.BlockSpec((None, CONV_HALO, d), lambda bi, i: (bi, jnp.minimum((i + 1) * hb, n_halo - 1), 0))
    vec = _resident((1, d))
    return pl.pallas_call(
        functools.partial(_conv_dw_body, tm=tm, n_tiles=n_tiles, row_chunk=128, lane_chunk=256),
        grid=(b, n_tiles),
        in_specs=[prev, main, nxt, main, _resident((CONV_WIDTH, d)), vec, vec, vec, _resident((d, d)), vec],
        out_specs=main,
        out_shape=jax.ShapeDtypeStruct((b, s, d), F32),
        scratch_shapes=[pltpu.VMEM((tm + 2 * CONV_HALO, d), F32), pltpu.VMEM((tm, d), F32)],
        compiler_params=_params("parallel", "parallel"),
        name="conv_dw",
    )(z, z, z, h, w_dw, b_dw, ln_g, ln_b, w2, b2)


def _conv_mixer(h, g, w_pw1, b_pw1, w_dw, b_dw, ln_g, ln_b, w_pw2, b_pw2):
    b, s, d = h.shape
    z = _conv_glu(h.reshape(b * s, d), g, w_pw1.astype(BF16), b_pw1.reshape(1, -1))
    return _conv_dw(z.reshape(b, s, d), h, w_dw, b_dw.reshape(1, d), ln_g.reshape(1, d), ln_b.reshape(1, d),
                    w_pw2.astype(BF16), b_pw2.reshape(1, d))


def _fft1_body(x_ref, g_ref, cs1_ref, ct_ref, st_ref, yr_ref, yi_ref, *, n2_per_step):
    n1 = x_ref.shape[0]
    d = g_ref.shape[1]
    for j in range(n2_per_step):
        sl = slice(j * d, (j + 1) * d)
        u = _rms(x_ref[:, sl], g_ref[...]).astype(BF16)
        y = _dot(cs1_ref[...], u)
        yr, ys = y[:n1], y[n1:]
        ct = jnp.concatenate([ct_ref[j]] * (d // LANES), axis=1)
        st = jnp.concatenate([st_ref[j]] * (d // LANES), axis=1)
        yr_ref[:, sl] = (yr * ct - ys * st).astype(BF16)
        yi_ref[:, sl] = (-(yr * st + ys * ct)).astype(BF16)


def _fft2_body(yr_ref, yi_ref, h_ref, m2_ref, cc_ref, sc_ref, wf_ref, bf_ref, o_ref):
    n2 = yr_ref.shape[0]
    ycat = jnp.concatenate([yr_ref[...], yi_ref[...]], axis=0)
    z = _dot(m2_ref[...], ycat)
    zr = z[:n2].astype(BF16)
    zi = z[n2:].astype(BF16)
    gd = FOURIER_GROUP_DIM
    parts = []
    for c0 in range(0, zr.shape[1], gd):
        parts.append(_dot(zr[:, c0:c0 + gd], cc_ref[...]) + _dot(zi[:, c0:c0 + gd], sc_ref[...]))
    mixed = jnp.concatenate(parts, axis=1).astype(BF16)
    o_ref[...] = h_ref[...] + _dot(mixed, wf_ref[...]) + bf_ref[...]


def _dft_tables(s):
    n2 = 128
    n1 = s // n2
    def cs(n):
        ang = 2.0 * np.pi * (np.outer(np.arange(n), np.arange(n)) % n) / n
        return np.cos(ang), np.sin(ang)
    c1, s1 = cs(n1)
    c2, s2 = cs(n2)
    cc, sc = cs(FOURIER_GROUP_DIM)
    scale = 1.0 / math.sqrt(s * FOURIER_GROUP_DIM)
    ang_t = 2.0 * np.pi * (np.outer(np.arange(n2), np.arange(n1)) % s) / s
    return dict(
        n1=n1, n2=n2,
        cs1=jnp.asarray(np.concatenate([c1, s1], axis=0), F32).astype(BF16),
        m2=jnp.asarray(np.block([[c2, s2], [-s2, c2]]), F32).astype(BF16),
        cc=jnp.asarray(cc * scale, F32).astype(BF16),
        sc=jnp.asarray(sc * scale, F32).astype(BF16),
        ct=jnp.asarray(np.cos(ang_t), F32), st=jnp.asarray(np.sin(ang_t), F32))


def _fourier_mixer(h, g, w_f, b_f, n2_per_step=8):
    b, s, d = h.shape
    tb = _dft_tables(s)
    n1, n2 = tb["n1"], tb["n2"]
    ct = jnp.broadcast_to(tb["ct"][:, :, None], (n2, n1, LANES))
    st = jnp.broadcast_to(tb["st"][:, :, None], (n2, n1, LANES))
    blk = pl.BlockSpec((None, n1, n2_per_step * d), lambda bi, j: (bi, 0, j))
    tw = pl.BlockSpec((n2_per_step, n1, LANES), lambda bi, j: (j, 0, 0))
    yr, yi = pl.pallas_call(
        functools.partial(_fft1_body, n2_per_step=n2_per_step),
        grid=(b, n2 // n2_per_step),
        in_specs=[blk, _resident((1, d)), _resident((2 * n1, n1)), tw, tw],
        out_specs=[blk, blk],
        out_shape=[jax.ShapeDtypeStruct((b, n1, n2 * d), BF16)] * 2,
        compiler_params=_params("parallel", "parallel"),
        name="fft_stage1",
    )(h.reshape(b, n1, n2 * d), g, tb["cs1"], ct, st)
    yblk = pl.BlockSpec((None, None, n2, d), lambda bi, k1: (bi, k1, 0, 0))
    hblk = pl.BlockSpec((None, n2, d), lambda bi, k1: (bi, 0, k1))
    gd = FOURIER_GROUP_DIM
    out = pl.pallas_call(
        _fft2_body,
        grid=(b, n1),
        in_specs=[yblk, yblk, hblk, _resident((2 * n2, 2 * n2)), _resident((gd, gd)), _resident((gd, gd)),
                  _resident((d, d)), _resident((1, d))],
        out_specs=hblk,
        out_shape=jax.ShapeDtypeStruct((b, n2, n1 * d), F32),
        compiler_params=_params("parallel", "parallel"),
        name="fft_stage2",
    )(yr.reshape(b, n1, n2, d), yi.reshape(b, n1, n2, d), h.reshape(b, n2, n1 * d),
      tb["m2"], tb["cc"], tb["sc"], w_f.astype(BF16), b_f.reshape(1, d))
    return out.reshape(b, s, d)


def _gla_proj_body(x_ref, g_ref, win_ref, wa1_ref, wa2_ref, ba_ref, q_ref, k_ref, v_ref, r_ref, la_ref):
    u = _rms(x_ref[...], g_ref[...]).astype(BF16)
    proj = _dot(u, win_ref[...])
    dk = q_ref.shape[1]
    dv = v_ref.shape[1]
    q_ref[...] = (proj[:, :dk] * (GLA_HEAD_K ** -0.5)).astype(BF16)
    k_ref[...] = proj[:, dk:2 * dk].astype(BF16)
    v_ref[...] = proj[:, 2 * dk:2 * dk + dv].astype(BF16)
    r_ref[...] = proj[:, 2 * dk + dv:].astype(BF16)
    low = _dot(u, wa1_ref[...]).astype(BF16)
    z = _dot(low, wa2_ref[...]) + ba_ref[...]
    log_sig = jnp.minimum(z, 0.0) - jnp.log(1.0 + jnp.exp(-jnp.abs(z)))
    la_ref[...] = log_sig / GLA_TAU


def _gla_proj(h2, g, w_in, wa1, wa2, ba, tm=512):
    t, d = h2.shape
    row = lambda n: pl.BlockSpec((tm, n), lambda i: (i, 0))
    rank2 = wa1.shape[1]
    return pl.pallas_call(
        _gla_proj_body,
        grid=(t // tm,),
        in_specs=[row(d), _resident((1, d)), _resident(w_in.shape), _resident((d, rank2)),
                  _resident((rank2, 2 * GLA_DK)), _resident((1, 2 * GLA_DK))],
        out_specs=[row(GLA_DK), row(GLA_DK), row(GLA_DV), row(GLA_DV), row(2 * GLA_DK)],
        out_shape=[jax.ShapeDtypeStruct((t, GLA_DK), BF16), jax.ShapeDtypeStruct((t, GLA_DK), BF16),
                   jax.ShapeDtypeStruct((t, GLA_DV), BF16), jax.ShapeDtypeStruct((t, GLA_DV), BF16),
                   jax.ShapeDtypeStruct((t, 2 * GLA_DK), F32)],
        compiler_params=_params("parallel"),
        name="gla_proj",
    )(h2, g, w_in, wa1, wa2, ba)


def _gla_chunk(q_ref, k_ref, v_ref, la_ref, o_ref, st_ref, rs, *, backward, tri, keep):
    c = GLA_CHUNK
    la = la_ref[rs, :]
    la_hi = la.astype(BF16)
    la_lo = (la - la_hi.astype(F32)).astype(BF16)
    cum = _dot(tri, la_hi) + _dot(tri, la_lo)
    if backward:
        ref, last = cum[c // 2:c // 2 + 1], cum[0:1]
    else:
        ref, last = cum[c // 2 - 1:c // 2], cum[c - 1:c]
    q = q_ref[rs, :].astype(F32)
    k = k_ref[rs, :].astype(F32)
    q_in = (q * jnp.exp(cum - ref)).astype(BF16)
    k_in = (k * jnp.exp(ref - cum)).astype(BF16)
    q_x = (q * jnp.exp(cum)).astype(BF16)
    k_x = (k * jnp.exp(last - cum)).astype(BF16)
    decay = jnp.exp(last)
    for hd in range(GLA_HEADS):
        ks = slice(GLA_HEAD_K * hd, GLA_HEAD_K * (hd + 1))
        vs = slice(GLA_HEAD_V * hd, GLA_HEAD_V * (hd + 1))
        v = v_ref[rs, vs]
        sc = lax.dot_general(q_in[:, ks], k_in[:, ks], NT_DIMS, preferred_element_type=F32)
        sc = jnp.where(keep, sc, 0.0).astype(BF16)
        state_t = st_ref[hd]
        o = _dot(sc, v) + lax.dot_general(q_x[:, ks], state_t.astype(BF16), NT_DIMS, preferred_element_type=F32)
        o_ref[rs, vs] = o
        st_ref[hd] = state_t * decay[:, ks] + lax.dot_general(v, k_x[:, ks], TN_DIMS, preferred_element_type=F32)


def _gla_scan_body(qf, kf, vf, laf, qb, kb, vb, lab, of_ref, ob_ref, stf_ref, stb_ref, *, chunks_per_step):
    @pl.when(pl.program_id(1) == 0)
    def _():
        stf_ref[...] = jnp.zeros_like(stf_ref)
        stb_ref[...] = jnp.zeros_like(stb_ref)
    c = GLA_CHUNK
    row = lax.broadcasted_iota(jnp.int32, (c, c), 0)
    col = lax.broadcasted_iota(jnp.int32, (c, c), 1)
    lower = col <= row
    upper = col >= row
    tri_f = jnp.where(lower, 1.0, 0.0).astype(BF16)
    tri_b = jnp.where(upper, 1.0, 0.0).astype(BF16)
    for step in range(chunks_per_step):
        _gla_chunk(qf, kf, vf, laf, of_ref, stf_ref, slice(step * c, (step + 1) * c),
                   backward=False, tri=tri_f, keep=lower)
        cb = chunks_per_step - 1 - step
        _gla_chunk(qb, kb, vb, lab, ob_ref, stb_ref, slice(cb * c, (cb + 1) * c),
                   backward=True, tri=tri_b, keep=col > row)


def _gla_scan(q, k, v, la, b, s, chunks_per_step=4):
    rows = chunks_per_step * GLA_CHUNK
    nb = s // rows
    fwd = lambda n, col=0: pl.BlockSpec((None, rows, n), lambda bi, i: (bi, i, col))
    bwd = lambda n, col=0: pl.BlockSpec((None, rows, n), lambda bi, i: (bi, nb - 1 - i, col))
    q3, k3, v3 = (a.reshape(b, s, a.shape[-1]) for a in (q, k, v))
    la3 = la.reshape(b, s, 2 * GLA_DK)
    state = pltpu.VMEM((GLA_HEADS, GLA_HEAD_V, GLA_HEAD_K), F32)
    return pl.pallas_call(
        functools.partial(_gla_scan_body, chunks_per_step=chunks_per_step),
        grid=(b, nb),
        in_specs=[fwd(GLA_DK), fwd(GLA_DK), fwd(GLA_DV), fwd(GLA_DK, 0),
                  bwd(GLA_DK), bwd(GLA_DK), bwd(GLA_DV), bwd(GLA_DK, 1)],
        out_specs=[fwd(GLA_DV), bwd(GLA_DV)],
        out_shape=[jax.ShapeDtypeStruct((b, s, GLA_DV), F32)] * 2,
        scratch_shapes=[state, state],
        compiler_params=_params("parallel", "arbitrary"),
        name="gla_scan",
    )(q3, k3, v3, la3, q3, k3, v3, la3)


def _gla_out_body(of_ref, ob_ref, r_ref, ng_ref, h_ref, w_ref, out_ref):
    o = of_ref[...] + ob_ref[...]
    parts = []
    for hd in range(GLA_HEADS):
        oh = o[:, GLA_HEAD_V * hd:GLA_HEAD_V * (hd + 1)]
        parts.append(oh * lax.rsqrt(jnp.mean(oh * oh, axis=-1, keepdims=True) + EPS))
    o = jnp.concatenate(parts, axis=1) * ng_ref[...] * _silu(r_ref[...].astype(F32))
    out_ref[...] = h_ref[...] + _dot(o.astype(BF16), w_ref[...])


def _gla_out(o_f, o_b, r, norm_g, h2, w_o, tm=512):
    t, d = h2.shape
    row = lambda n: pl.BlockSpec((tm, n), lambda i: (i, 0))
    return pl.pallas_call(
        _gla_out_body,
        grid=(t // tm,),
        in_specs=[row(GLA_DV), row(GLA_DV), row(GLA_DV), _resident((1, GLA_DV)), row(d), _resident((GLA_DV, d))],
        out_specs=row(d),
        out_shape=jax.ShapeDtypeStruct((t, d), F32),
        compiler_params=_params("parallel"),
        name="gla_out",
    )(o_f.reshape(t, GLA_DV), o_b.reshape(t, GLA_DV), r, norm_g, h2, w_o)


def _gla_mixer(h, g, w_in, w_a1, w_a2, b_a, norm_g, w_o):
    b, s, d = h.shape
    rank = w_a1.shape[-1]
    wa1 = jnp.concatenate([w_a1[0], w_a1[1]], axis=1).astype(BF16)
    zeros = jnp.zeros((rank, GLA_DK), F32)
    wa2 = jnp.concatenate([jnp.concatenate([w_a2[0], zeros], axis=1),
                           jnp.concatenate([zeros, w_a2[1]], axis=1)], axis=0).astype(BF16)
    ba = jnp.concatenate([b_a[0], b_a[1]]).reshape(1, 2 * GLA_DK)
    h2 = h.reshape(b * s, d)
    q, k, v, r, la = _gla_proj(h2, g, w_in.astype(BF16), wa1, wa2, ba)
    o_f, o_b = _gla_scan(q, k, v, la, b, s)
    return _gla_out(o_f, o_b, r, norm_g.reshape(1, GLA_DV), h2, w_o.astype(BF16)).reshape(b, s, d)


def kernel(x, norm_g, final_norm_g, ffn_w1, ffn_w3, ffn_w2, attn_w_qkv, attn_w_o, conv_w_pw1, conv_b_pw1,
           conv_w_dw, conv_b_dw, conv_ln_g, conv_ln_b, conv_w_pw2, conv_b_pw2, fnet_w, fnet_b, gla_w_in,
           gla_w_a1, gla_w_a2, gla_b_a, gla_norm_g, gla_w_o):
    b, s, d = x.shape
    depth = norm_g.shape[0]
    n_mixers = 4
    gf = final_norm_g.reshape(1, d)
    h = x

    def ffn(h, i, half, final=False):
        out = _ffn(h.reshape(b * s, d), norm_g[i, 2 * half].reshape(1, d), ffn_w1[i, half].astype(BF16),
                   ffn_w3[i, half].astype(BF16), ffn_w2[i, half].astype(BF16), gf, final=final)
        return out.reshape(b, s, d)

    for i in range(depth):
        m, j = i % n_mixers, i // n_mixers
        h = ffn(h, i, 0)
        g = norm_g[i, 1].reshape(1, d)
        if m == 0:
            h = _attention_mixer(h, g, attn_w_qkv[j], attn_w_o[j])
        elif m == 1:
            h = _conv_mixer(h, g, conv_w_pw1[j], conv_b_pw1[j], conv_w_dw[j], conv_b_dw[j], conv_ln_g[j],
                            conv_ln_b[j], conv_w_pw2[j], conv_b_pw2[j])
        elif m == 2:
            h = _fourier_mixer(h, g, fnet_w[j], fnet_b[j])
        else:
            h = _gla_mixer(h, g, gla_w_in[j], gla_w_a1[j], gla_w_a2[j], gla_b_a[j], gla_norm_g[j], gla_w_o[j])
        h = ffn(h, i, 1, final=(i == depth - 1))
    return h
```

```python
import functools
import math

import numpy as np
import jax
import jax.numpy as jnp
from jax import lax
from jax.experimental import pallas as pl
from jax.experimental.pallas import tpu as pltpu

F32 = jnp.float32
BF16 = jnp.bfloat16
EPS = 1e-6
MASK_VALUE = -1e30

LANES = 128
SUBLANES = 8
VMEM_LIMIT_BYTES = 56 * 1024 * 1024

DIL_PATTERNS = ((128, 1), (512, 4), (2048, 16))
HEADS_PER_GROUP = 8
HEAD_DIM_A = 64
ATTN_WIDTH = HEADS_PER_GROUP * HEAD_DIM_A
BAND_RADIUS = 64
CONV_WIDTH = 31
CONV_HALO = 16
FOURIER_GROUP_DIM = 256
GLA_HEADS = 4
GLA_HEAD_K = 128
GLA_HEAD_V = 256
GLA_DK = GLA_HEADS * GLA_HEAD_K
GLA_DV = GLA_HEADS * GLA_HEAD_V
GLA_TAU = 16.0
GLA_CHUNK = 64

NT_DIMS = (((1,), (1,)), ((), ()))
TN_DIMS = (((0,), (0,)), ((), ()))


def _params(*semantics):
    return pltpu.CompilerParams(dimension_semantics=semantics, vmem_limit_bytes=VMEM_LIMIT_BYTES)


def _resident(shape):
    nd = len(shape)
    return pl.BlockSpec(shape, lambda *_: (0,) * nd, pipeline_mode=pl.Buffered(1))


def _rms(x, g):
    ms = jnp.mean(x * x, axis=-1, keepdims=True)
    return x * lax.rsqrt(ms + EPS) * g


def _silu(x):
    return x * jax.nn.sigmoid(x)


def _dot(a, b):
    return jnp.dot(a, b, preferred_element_type=F32)


def _ffn_body(x_ref, g_ref, w1_ref, w3_ref, w2_ref, gf_ref, o_ref, gate_ref, *, ff_chunk, final):
    x = x_ref[...]
    hn = _rms(x, g_ref[...]).astype(BF16)
    d_ff = w1_ref.shape[1]
    for c0 in range(0, d_ff, ff_chunk):
        sl = slice(c0, c0 + ff_chunk)
        a = _dot(hn, w1_ref[:, sl])
        b = _dot(hn, w3_ref[:, sl])
        gate_ref[:, sl] = (_silu(a) * b).astype(BF16)
    y = x + 0.5 * _dot(gate_ref[...], w2_ref[...])
    if final:
        y = _rms(y, gf_ref[...])
    o_ref[...] = y


def _ffn(h2, g, w1, w3, w2, gf, *, final, tm=512, ff_chunk=256):
    t, d = h2.shape
    d_ff = w1.shape[1]
    row = pl.BlockSpec((tm, d), lambda i: (i, 0))
    return pl.pallas_call(
        functools.partial(_ffn_body, ff_chunk=ff_chunk, final=final),
        grid=(t // tm,),
        in_specs=[row, _resident((1, d)), _resident((d, d_ff)), _resident((d, d_ff)),
                  _resident((d_ff, d)), _resident((1, d))],
        out_specs=row,
        out_shape=jax.ShapeDtypeStruct((t, d), F32),
        scratch_shapes=[pltpu.VMEM((tm, d_ff), BF16)],
        compiler_params=_params("parallel"),
        name="ffn_final" if final else "ffn",
    )(h2, g, w1, w3, w2, gf)


def _attn_proj_body(x_ref, g_ref, w_ref, o1_ref, o4_ref, o16_ref, y_ref, *, tm):
    u = _rms(x_ref[...], g_ref[...]).astype(BF16)
    n = 3 * ATTN_WIDTH
    o_refs = (o1_ref, o4_ref, o16_ref)

    def project(grp):
        return _dot(u, w_ref[:, grp * n:(grp + 1) * n])

    def emit(grp, y):
        o_ref, dil = o_refs[grp], DIL_PATTERNS[grp][1]
        if dil == 1:
            o_ref[0] = y.astype(BF16)
            return
        for c in range(n // LANES):
            y_ref[c] = y[:, c * LANES:(c + 1) * LANES]
        for r in range(dil):
            for c in range(n // LANES):
                o_ref[r, :, c * LANES:(c + 1) * LANES] = y_ref[c, pl.ds(r, tm // dil, stride=dil), :].astype(BF16)

    y = project(0)
    for grp in range(len(DIL_PATTERNS)):
        y_next = project(grp + 1) if grp + 1 < len(DIL_PATTERNS) else None
        emit(grp, y)
        y = y_next


def _attn_proj(h, g, w, tm=512):
    b, s, d = h.shape
    n = 3 * ATTN_WIDTH
    out_specs, out_shape = [], []
    for _, dil in DIL_PATTERNS:
        out_specs.append(pl.BlockSpec((None, dil, tm // dil, n), lambda bi, i: (bi, 0, i, 0)))
        out_shape.append(jax.ShapeDtypeStruct((b, dil, s // dil, n), BF16))
    return pl.pallas_call(
        functools.partial(_attn_proj_body, tm=tm),
        grid=(b, s // tm),
        in_specs=[pl.BlockSpec((None, tm, d), lambda bi, i: (bi, i, 0)), _resident((1, d)),
                  _resident((d, len(DIL_PATTERNS) * n))],
        out_specs=out_specs,
        out_shape=out_shape,
        scratch_shapes=[pltpu.VMEM((n // LANES, tm, LANES), F32)],
        compiler_params=_params("parallel", "parallel"),
        name="attn_proj",
    )(h, g, w)


def _band_body(q_ref, kp_ref, km_ref, kn_ref, vp_ref, vm_ref, vn_ref, o_ref, lse_ref, *, tq, sub, slopes):
    i = pl.program_id(2)
    rad = BAND_RADIUS
    nk = 3 * rad
    n_sb = tq // rad
    r = lax.broadcasted_iota(jnp.int32, (rad, nk), 0)
    c = lax.broadcasted_iota(jnp.int32, (rad, nk), 1)
    dist = jnp.abs(c - rad - r)
    absrel = dist.astype(F32)
    out_of_band = jnp.where(dist <= rad, 0.0, -MASK_VALUE)
    col = lax.broadcasted_iota(jnp.int32, (1, nk), 1)
    is_first = (i == 0).astype(F32)
    is_last = ((i + 1) * tq >= sub).astype(F32)
    before_start = jnp.where(col < rad, -MASK_VALUE, 0.0) * is_first
    after_end = jnp.where(col >= 2 * rad, -MASK_VALUE, 0.0) * is_last
    lane = lax.broadcasted_iota(jnp.int32, (rad, LANES), 1)
    low = lane < HEAD_DIM_A
    lowf = low.astype(F32)
    n_pairs = HEADS_PER_GROUP // 2

    def scores(p):
        sl = slice(LANES * p, LANES * (p + 1))
        kc = jnp.concatenate([kp_ref[:, sl], km_ref[:, sl], kn_ref[:, sl]], axis=0)
        pen = jnp.concatenate([slopes[2 * p + hh] * absrel + out_of_band for hh in range(2)], axis=0)
        out = []
        for sb in range(n_sb):
            q2 = q_ref[sb * rad:(sb + 1) * rad, sl].astype(F32)
            qm = jnp.concatenate([q2 * lowf, q2 * (1.0 - lowf)], axis=0).astype(BF16)
            sc = lax.dot_general(qm, kc[sb * rad:sb * rad + nk], NT_DIMS, preferred_element_type=F32) - pen
            if sb == 0:
                sc = sc - before_start
            if sb == n_sb - 1:
                sc = sc - after_end
            out.append(sc)
        return out

    def finish(p, scs):
        sl = slice(LANES * p, LANES * (p + 1))
        vc = jnp.concatenate([vp_ref[:, sl], vm_ref[:, sl], vn_ref[:, sl]], axis=0)
        for sb, sc in enumerate(scs):
            rows = slice(sb * rad, (sb + 1) * rad)
            m = jnp.max(sc, axis=-1, keepdims=True)
            e = jnp.exp(sc - m)
            l = jnp.sum(e, axis=-1, keepdims=True)
            o = _dot(e.astype(BF16), vc[sb * rad:sb * rad + nk]) / l
            lse = m + jnp.log(l)
            o_ref[rows, sl] = jnp.where(low, o[:rad], o[rad:]).astype(BF16)
            lse_ref[rows, sl] = jnp.where(low, lse[:rad], lse[rad:])

    pending = scores(0)
    for p in range(1, n_pairs):
        upcoming = scores(p)
        finish(p - 1, pending)
        pending = upcoming
    finish(n_pairs - 1, pending)


def _band_attention(qkv, group, dil, tq=256):
    b, _, sub, _ = qkv.shape
    w = ATTN_WIDTH
    tq = min(tq, sub)
    hb = tq // BAND_RADIUS
    n_halo = sub // BAND_RADIUS
    n_heads = len(DIL_PATTERNS) * HEADS_PER_GROUP
    slopes = tuple(float(2.0 ** (-8.0 * (group * HEADS_PER_GROUP + hd + 1) / n_heads)) * dil
                   for hd in range(HEADS_PER_GROUP))

    def main(col):
        return pl.BlockSpec((None, None, tq, w), lambda bi, r, i: (bi, r, i, col))

    def prev(col):
        return pl.BlockSpec((None, None, BAND_RADIUS, w),
                            lambda bi, r, i: (bi, r, jnp.maximum(i * hb - 1, 0), col))

    def nxt(col):
        return pl.BlockSpec((None, None, BAND_RADIUS, w),
                            lambda bi, r, i: (bi, r, jnp.minimum((i + 1) * hb, n_halo - 1), col))

    return pl.pallas_call(
        functools.partial(_band_body, tq=tq, sub=sub, slopes=slopes),
        grid=(b, dil, sub // tq),
        in_specs=[main(0), prev(1), main(1), nxt(1), prev(2), main(2), nxt(2)],
        out_specs=[main(0), main(0)],
        out_shape=[jax.ShapeDtypeStruct((b, dil, sub, w), BF16), jax.ShapeDtypeStruct((b, dil, sub, w), F32)],
        compiler_params=_params("parallel", "parallel", "parallel"),
        name=f"band_attn_d{dil}",
    )(qkv, qkv, qkv, qkv, qkv, qkv, qkv)


def _attn_out_body(o1, o4, o16, l1, l4, l16, h_ref, w_ref, out_ref, os_ref, ls_ref, *, tm):
    n_lane_groups = ATTN_WIDTH // LANES
    for grp, (o_ref, l_ref, (_, dil)) in enumerate(zip((o1, o4, o16), (l1, l4, l16), DIL_PATTERNS)):
        for r in range(dil):
            rows = pl.ds(r, tm // dil, stride=dil) if dil > 1 else slice(None)
            for c in range(n_lane_groups):
                cs = slice(c * LANES, (c + 1) * LANES)
                os_ref[grp, c, rows, :] = o_ref[r, :, cs].astype(F32)
                ls_ref[grp, c, rows, :] = l_ref[r, :, cs]

    def gathered(ref, grp):
        return jnp.concatenate([ref[grp, c] for c in range(n_lane_groups)], axis=1)

    ls = [gathered(ls_ref, grp) for grp in range(len(DIL_PATTERNS))]
    m = jnp.maximum(jnp.maximum(ls[0], ls[1]), ls[2])
    es = [jnp.exp(l - m) for l in ls]
    num = es[0] * gathered(os_ref, 0) + es[1] * gathered(os_ref, 1) + es[2] * gathered(os_ref, 2)
    o = (num / (es[0] + es[1] + es[2])).astype(BF16)
    out_ref[...] = h_ref[...] + _dot(o, w_ref[...])


def _attn_out(os_, lses, h, w_o, tm=512):
    b, s, d = h.shape
    w = ATTN_WIDTH
    specs = [pl.BlockSpec((None, dil, tm // dil, w), lambda bi, i: (bi, 0, i, 0)) for _, dil in DIL_PATTERNS]
    row = pl.BlockSpec((None, tm, d), lambda bi, i: (bi, i, 0))
    n_grp = len(DIL_PATTERNS)
    return pl.pallas_call(
        functools.partial(_attn_out_body, tm=tm),
        grid=(b, s // tm),
        in_specs=specs + specs + [row, _resident((w, d))],
        out_specs=row,
        out_shape=jax.ShapeDtypeStruct((b, s, d), F32),
        scratch_shapes=[pltpu.VMEM((n_grp, w // LANES, tm, LANES), F32)] * 2,
        compiler_params=_params("parallel", "parallel"),
        name="attn_out",
    )(*os_, *lses, h, w_o)


def _attention_mixer(h, g, w_qkv, w_o):
    w = ATTN_WIDTH
    col = jnp.arange(w_qkv.shape[1]) % (3 * w)
    wq = (w_qkv * jnp.where(col < w, HEAD_DIM_A ** -0.5, 1.0)).astype(BF16)
    qkvs = _attn_proj(h, g, wq)
    os_, lses = [], []
    for grp, (_, dil) in enumerate(DIL_PATTERNS):
        o, lse = _band_attention(qkvs[grp], grp, dil)
        os_.append(o)
        lses.append(lse)
    return _attn_out(os_, lses, h, w_o.astype(BF16))


def _conv_body(xp_ref, xm_ref, xn_ref, g_ref, w1_ref, b1_ref, wdw_ref, bdw_ref, lng_ref, lnb_ref, w2_ref, b2_ref,
               o_ref, xext_ref, zext_ref, zs_ref, cv_ref, *, tm, n_tiles, row_chunk, lane_chunk):
    i = pl.program_id(1)
    d = xm_ref.shape[1]
    halo = CONV_HALO
    ext = tm + 2 * halo
    xext_ref[0:halo, :] = xp_ref[...]
    xext_ref[halo:halo + tm, :] = xm_ref[...]
    xext_ref[halo + tm:ext, :] = xn_ref[...]
    u = _rms(xext_ref[...], g_ref[...]).astype(BF16)
    z = _dot(u, w1_ref[...]) + b1_ref[...]
    zext_ref[...] = z[:, :d] * jax.nn.sigmoid(z[:, d:])
    zext_ref[0:halo, :] = jnp.where(i > 0, zext_ref[0:halo, :], 0.0)
    zext_ref[halo + tm:ext, :] = jnp.where(i < n_tiles - 1, zext_ref[halo + tm:ext, :], 0.0)
    span = ext - SUBLANES
    for r in range(1, SUBLANES):
        for c0 in range(0, d, LANES):
            zs_ref[r - 1, :, c0:c0 + LANES] = zext_ref[r:r + span, c0:c0 + LANES]
    first = halo - CONV_WIDTH // 2
    for c0 in range(0, d, lane_chunk):
        cs = slice(c0, c0 + lane_chunk)
        for r0 in range(0, tm, row_chunk):
            acc = jnp.zeros((row_chunk, lane_chunk), F32)
            for k in range(CONV_WIDTH):
                q, r = divmod(first + k, SUBLANES)
                rows = slice(r0 + q * SUBLANES, r0 + q * SUBLANES + row_chunk)
                tap = zext_ref[rows, cs] if r == 0 else zs_ref[r - 1, rows, cs]
                acc = acc + wdw_ref[k:k + 1, cs] * tap
            cv_ref[r0:r0 + row_chunk, cs] = acc + bdw_ref[:, cs]
    z = cv_ref[...]
    mu = jnp.mean(z, axis=-1, keepdims=True)
    zc = z - mu
    var = jnp.mean(zc * zc, axis=-1, keepdims=True)
    zn = zc * lax.rsqrt(var + EPS) * lng_ref[...] + lnb_ref[...]
    y = _silu(zn).astype(BF16)
    o_ref[...] = xm_ref[...] + _dot(y, w2_ref[...]) + b2_ref[...]


def _conv_mixer(h, g, w_pw1, b_pw1, w_dw, b_dw, ln_g, ln_b, w_pw2, b_pw2, tm=256):
    b, s, d = h.shape
    n_tiles = s // tm
    hb = tm // CONV_HALO
    n_halo = s // CONV_HALO
    ext = tm + 2 * CONV_HALO
    main = pl.BlockSpec((None, tm, d), lambda bi, i: (bi, i, 0))
    prev = pl.BlockSpec((None, CONV_HALO, d), lambda bi, i: (bi, jnp.maximum(i * hb - 1, 0), 0))
    nxt = pl.BlockSpec((None, CONV_HALO, d), lambda bi, i: (bi, jnp.minimum((i + 1) * hb, n_halo - 1), 0))
    vec = _resident((1, d))
    return pl.pallas_call(
        functools.partial(_conv_body, tm=tm, n_tiles=n_tiles, row_chunk=128, lane_chunk=256),
        grid=(b, n_tiles),
        in_specs=[prev, main, nxt, vec, _resident((d, 2 * d)), _resident((1, 2 * d)), _resident((CONV_WIDTH, d)),
                  vec, vec, vec, _resident((d, d)), vec],
        out_specs=main,
        out_shape=jax.ShapeDtypeStruct((b, s, d), F32),
        scratch_shapes=[pltpu.VMEM((ext, d), F32), pltpu.VMEM((ext, d), F32),
                        pltpu.VMEM((SUBLANES - 1, ext - SUBLANES, d), F32), pltpu.VMEM((tm, d), F32)],
        compiler_params=_params("parallel", "parallel"),
        name="conv_module",
    )(h, h, h, g, w_pw1.astype(BF16), b_pw1.reshape(1, 2 * d), w_dw, b_dw.reshape(1, d), ln_g.reshape(1, d),
      ln_b.reshape(1, d), w_pw2.astype(BF16), b_pw2.reshape(1, d))


def _pack_complex(re, im):
    hi = lax.bitcast_convert_type(re.astype(BF16).astype(F32), jnp.uint32)
    lo = lax.bitcast_convert_type(im.astype(BF16).astype(F32), jnp.uint32)
    return hi | (lo >> 16)


def _unpack_complex(word):
    re = lax.bitcast_convert_type(word & jnp.uint32(0xFFFF0000), F32)
    im = lax.bitcast_convert_type(word << 16, F32)
    return re.astype(BF16), im.astype(BF16)


def _fft1_body(x_ref, g_ref, w1_ref, ct_ref, st_ref, y_ref):
    n1, n2s, d = x_ref.shape
    rows = n1 * n2s
    u = _rms(x_ref[...].reshape(rows, d), g_ref[...]).astype(BF16)
    y = _dot(w1_ref[...], u)
    yr, ys = y[:rows], y[rows:]
    ct = jnp.concatenate([ct_ref[...].reshape(rows, LANES)] * (d // LANES), axis=1)
    st = jnp.concatenate([st_ref[...].reshape(rows, LANES)] * (d // LANES), axis=1)
    y_ref[...] = _pack_complex(yr * ct - ys * st, -(yr * st + ys * ct)).reshape(n1, n2s, d)


def _fft2_body(y_ref, h_ref, m2_ref, cc_ref, sc_ref, wf_ref, bf_ref, o_ref, zr_ref, zi_ref, mix_ref, res_ref, *,
               out_rows):
    k1s, n2, d = y_ref.shape
    for jj in range(k1s):
        re, im = _unpack_complex(y_ref[jj])
        z = _dot(m2_ref[...], jnp.concatenate([re, im], axis=0))
        zr_ref[jj * n2:(jj + 1) * n2, :] = z[:n2].astype(BF16)
        zi_ref[jj * n2:(jj + 1) * n2, :] = z[n2:].astype(BF16)
    gd = FOURIER_GROUP_DIM
    for c0 in range(0, d, gd):
        cs = slice(c0, c0 + gd)
        mix_ref[:, cs] = (_dot(zr_ref[:, cs], cc_ref[...]) + _dot(zi_ref[:, cs], sc_ref[...])).astype(BF16)
    for r0 in range(0, k1s * n2, out_rows):
        res = _dot(mix_ref[r0:r0 + out_rows, :], wf_ref[...]) + bf_ref[...]
        for c in range(d // LANES):
            res_ref[c, r0:r0 + out_rows, :] = res[:, c * LANES:(c + 1) * LANES]

    def emit(k2, carry):
        for c in range(d // LANES):
            cs = slice(c * LANES, (c + 1) * LANES)
            o_ref[k2, :, cs] = h_ref[k2, :, cs] + res_ref[c, pl.ds(k2, k1s, stride=n2), :]
        return carry

    lax.fori_loop(0, n2, emit, 0)


def _dft_tables(s, n2_per_step):
    n2 = 128
    n1 = s // n2
    def cs(n):
        ang = 2.0 * np.pi * (np.outer(np.arange(n), np.arange(n)) % n) / n
        return np.cos(ang), np.sin(ang)
    c1, s1 = cs(n1)
    c2, s2 = cs(n2)
    cc, sc = cs(FOURIER_GROUP_DIM)
    eye = np.eye(n2_per_step)
    scale = 1.0 / math.sqrt(s * FOURIER_GROUP_DIM)
    ang_t = 2.0 * np.pi * (np.outer(np.arange(n1), np.arange(n2)) % s) / s
    return dict(
        n1=n1, n2=n2,
        w1=jnp.asarray(np.concatenate([np.kron(c1, eye), np.kron(s1, eye)], axis=0), F32).astype(BF16),
        m2=jnp.asarray(np.block([[c2, s2], [-s2, c2]]), F32).astype(BF16),
        cc=jnp.asarray(cc * scale, F32).astype(BF16),
        sc=jnp.asarray(sc * scale, F32).astype(BF16),
        ct=jnp.asarray(np.cos(ang_t), F32), st=jnp.asarray(np.sin(ang_t), F32))


def _fourier_mixer(h, g, w_f, b_f, n2_per_step=SUBLANES, k1_per_step=SUBLANES, out_rows=256):
    b, s, d = h.shape
    tb = _dft_tables(s, n2_per_step)
    n1, n2 = tb["n1"], tb["n2"]
    ct = jnp.broadcast_to(tb["ct"][:, :, None], (n1, n2, LANES))
    st = jnp.broadcast_to(tb["st"][:, :, None], (n1, n2, LANES))
    rows1 = n1 * n2_per_step
    xblk = pl.BlockSpec((None, n1, n2_per_step, d), lambda bi, j: (bi, 0, j, 0))
    tw = pl.BlockSpec((n1, n2_per_step, LANES), lambda bi, j: (0, j, 0))
    y = pl.pallas_call(
        _fft1_body,
        grid=(b, n2 // n2_per_step),
        in_specs=[xblk, _resident((1, d)), _resident((2 * rows1, rows1)), tw, tw],
        out_specs=xblk,
        out_shape=jax.ShapeDtypeStruct((b, n1, n2, d), jnp.uint32),
        compiler_params=_params("parallel", "parallel"),
        name="fft_stage1",
    )(h.reshape(b, n1, n2, d), g, tb["w1"], ct, st)
    gd = FOURIER_GROUP_DIM
    rows2 = k1_per_step * n2
    oblk = pl.BlockSpec((None, n2, k1_per_step, d), lambda bi, k: (bi, 0, k, 0))
    out = pl.pallas_call(
        functools.partial(_fft2_body, out_rows=out_rows),
        grid=(b, n1 // k1_per_step),
        in_specs=[pl.BlockSpec((None, k1_per_step, n2, d), lambda bi, k: (bi, k, 0, 0)), oblk,
                  _resident((2 * n2, 2 * n2)), _resident((gd, gd)), _resident((gd, gd)),
                  _resident((d, d)), _resident((1, d))],
        out_specs=oblk,
        out_shape=jax.ShapeDtypeStruct((b, n2, n1, d), F32),
        scratch_shapes=[pltpu.VMEM((rows2, d), BF16)] * 3 + [pltpu.VMEM((d // LANES, rows2, LANES), F32)],
        compiler_params=_params("parallel", "parallel"),
        name="fft_stage2",
    )(y, h.reshape(b, n2, n1, d), tb["m2"], tb["cc"], tb["sc"], w_f.astype(BF16), b_f.reshape(1, d))
    return out.reshape(b, s, d)


def _gla_proj_body(x_ref, g_ref, win_ref, wa1_ref, wa2_ref, ba_ref, q_ref, k_ref, v_ref, r_ref, la_ref):
    u = _rms(x_ref[...], g_ref[...]).astype(BF16)
    proj = _dot(u, win_ref[...])
    dk = q_ref.shape[1]
    dv = v_ref.shape[1]
    q_ref[...] = (proj[:, :dk] * (GLA_HEAD_K ** -0.5)).astype(BF16)
    k_ref[...] = proj[:, dk:2 * dk].astype(BF16)
    v_ref[...] = proj[:, 2 * dk:2 * dk + dv].astype(BF16)
    r_ref[...] = proj[:, 2 * dk + dv:].astype(BF16)
    low = _dot(u, wa1_ref[...]).astype(BF16)
    z = _dot(low, wa2_ref[...]) + ba_ref[...]
    log_sig = jnp.minimum(z, 0.0) - jnp.log(1.0 + jnp.exp(-jnp.abs(z)))
    la_ref[...] = log_sig / GLA_TAU


def _gla_proj(h2, g, w_in, wa1, wa2, ba, tm=512):
    t, d = h2.shape
    row = lambda n: pl.BlockSpec((tm, n), lambda i: (i, 0))
    rank2 = wa1.shape[1]
    return pl.pallas_call(
        _gla_proj_body,
        grid=(t // tm,),
        in_specs=[row(d), _resident((1, d)), _resident(w_in.shape), _resident((d, rank2)),
                  _resident((rank2, 2 * GLA_DK)), _resident((1, 2 * GLA_DK))],
        out_specs=[row(GLA_DK), row(GLA_DK), row(GLA_DV), row(GLA_DV), row(2 * GLA_DK)],
        out_shape=[jax.ShapeDtypeStruct((t, GLA_DK), BF16), jax.ShapeDtypeStruct((t, GLA_DK), BF16),
                   jax.ShapeDtypeStruct((t, GLA_DV), BF16), jax.ShapeDtypeStruct((t, GLA_DV), BF16),
                   jax.ShapeDtypeStruct((t, 2 * GLA_DK), F32)],
        compiler_params=_params("parallel"),
        name="gla_proj",
    )(h2, g, w_in, wa1, wa2, ba)


def _gla_prepare(q_ref, k_ref, la_ref, *, backward, n_chunks):
    c = GLA_CHUNK
    rows = n_chunks * c
    ri = lax.broadcasted_iota(jnp.int32, (rows, rows), 0)
    ci = lax.broadcasted_iota(jnp.int32, (rows, rows), 1)
    same_chunk = (ri // c) == (ci // c)
    if backward:
        tri = same_chunk & (ci >= ri)
        keep = same_chunk & (ci > ri)
    else:
        tri = same_chunk & (ci <= ri)
        keep = tri
    tri = jnp.where(tri, 1.0, 0.0).astype(BF16)
    la = la_ref[...]
    la_hi = la.astype(BF16)
    la_lo = (la - la_hi.astype(F32)).astype(BF16)
    cum = _dot(tri, la_hi) + _dot(tri, la_lo)
    ref_row = c // 2 if backward else c // 2 - 1
    last_row = 0 if backward else c - 1
    lasts = [cum[a * c + last_row:a * c + last_row + 1] for a in range(n_chunks)]
    ref = jnp.concatenate([jnp.broadcast_to(cum[a * c + ref_row:a * c + ref_row + 1], (c, cum.shape[1]))
                           for a in range(n_chunks)], axis=0)
    last = jnp.concatenate([jnp.broadcast_to(l, (c, cum.shape[1])) for l in lasts], axis=0)
    q = q_ref[...].astype(F32)
    k = k_ref[...].astype(F32)
    return dict(
        keep=keep, lasts=lasts, order=range(n_chunks - 1, -1, -1) if backward else range(n_chunks),
        q_in=(q * jnp.exp(cum - ref)).astype(BF16), k_in=(k * jnp.exp(ref - cum)).astype(BF16),
        q_x=(q * jnp.exp(cum)).astype(BF16), k_x=(k * jnp.exp(last - cum)).astype(BF16))


def _gla_scan_body(qf, kf, vf, laf, qb, kb, vb, lab, of_ref, ob_ref, stf_ref, stb_ref, accf_ref, accb_ref, *,
                   chunks_per_step):
    @pl.when(pl.program_id(1) == 0)
    def _():
        stf_ref[...] = jnp.zeros_like(stf_ref)
        stb_ref[...] = jnp.zeros_like(stb_ref)
    c = GLA_CHUNK
    heads = [(slice(GLA_HEAD_K * hd, GLA_HEAD_K * (hd + 1)), slice(GLA_HEAD_V * hd, GLA_HEAD_V * (hd + 1)))
             for hd in range(GLA_HEADS)]
    dirs = [(_gla_prepare(qf, kf, laf, backward=False, n_chunks=chunks_per_step), vf, of_ref, stf_ref, accf_ref),
            (_gla_prepare(qb, kb, lab, backward=True, n_chunks=chunks_per_step), vb, ob_ref, stb_ref, accb_ref)]
    scores = [[lax.dot_general(p["q_in"][:, ks], p["k_in"][:, ks], NT_DIMS, preferred_element_type=F32)
               for ks, _ in heads] for p, *_ in dirs]
    for (p, v_ref, _, _, acc_ref), scs in zip(dirs, scores):
        for (_, vs), sc in zip(heads, scs):
            acc_ref[:, vs] = _dot(jnp.where(p["keep"], sc, 0.0).astype(BF16), v_ref[:, vs])
    states = [[st_ref[hd] for hd in range(GLA_HEADS)] for _, _, _, st_ref, _ in dirs]
    for step in range(chunks_per_step):
        for di, (p, v_ref, o_ref, _, acc_ref) in enumerate(dirs):
            a = p["order"][step]
            rs = slice(a * c, (a + 1) * c)
            for hd, (ks, vs) in enumerate(heads):
                state_t = states[di][hd]
                inter = lax.dot_general(p["q_x"][rs, ks], state_t.astype(BF16), NT_DIMS, preferred_element_type=F32)
                o_ref[rs, vs] = (acc_ref[rs, vs] + inter).astype(o_ref.dtype)
                states[di][hd] = state_t * jnp.exp(p["lasts"][a][:, ks]) + lax.dot_general(
                    v_ref[rs, vs], p["k_x"][rs, ks], TN_DIMS, preferred_element_type=F32)
    for di, (_, _, _, st_ref, _) in enumerate(dirs):
        for hd in range(GLA_HEADS):
            st_ref[hd] = states[di][hd]


def _gla_scan(q, k, v, la, b, s, chunks_per_step=4):
    rows = chunks_per_step * GLA_CHUNK
    nb = s // rows
    fwd = lambda n, col=0: pl.BlockSpec((None, rows, n), lambda bi, i: (bi, i, col))
    bwd = lambda n, col=0: pl.BlockSpec((None, rows, n), lambda bi, i: (bi, nb - 1 - i, col))
    q3, k3, v3 = (a.reshape(b, s, a.shape[-1]) for a in (q, k, v))
    la3 = la.reshape(b, s, 2 * GLA_DK)
    state = pltpu.VMEM((GLA_HEADS, GLA_HEAD_V, GLA_HEAD_K), F32)
    return pl.pallas_call(
        functools.partial(_gla_scan_body, chunks_per_step=chunks_per_step),
        grid=(b, nb),
        in_specs=[fwd(GLA_DK), fwd(GLA_DK), fwd(GLA_DV), fwd(GLA_DK, 0),
                  bwd(GLA_DK), bwd(GLA_DK), bwd(GLA_DV), bwd(GLA_DK, 1)],
        out_specs=[fwd(GLA_DV), bwd(GLA_DV)],
        out_shape=[jax.ShapeDtypeStruct((b, s, GLA_DV), BF16)] * 2,
        scratch_shapes=[state, state, pltpu.VMEM((rows, GLA_DV), F32), pltpu.VMEM((rows, GLA_DV), F32)],
        compiler_params=_params("parallel", "arbitrary"),
        name="gla_scan",
    )(q3, k3, v3, la3, q3, k3, v3, la3)


def _gla_out_body(of_ref, ob_ref, r_ref, ng_ref, h_ref, w_ref, out_ref):
    o = of_ref[...].astype(F32) + ob_ref[...].astype(F32)
    parts = []
    for hd in range(GLA_HEADS):
        oh = o[:, GLA_HEAD_V * hd:GLA_HEAD_V * (hd + 1)]
        parts.append(oh * lax.rsqrt(jnp.mean(oh * oh, axis=-1, keepdims=True) + EPS))
    o = jnp.concatenate(parts, axis=1) * ng_ref[...] * _silu(r_ref[...].astype(F32))
    out_ref[...] = h_ref[...] + _dot(o.astype(BF16), w_ref[...])


def _gla_out(o_f, o_b, r, norm_g, h2, w_o, tm=512):
    t, d = h2.shape
    row = lambda n: pl.BlockSpec((tm, n), lambda i: (i, 0))
    return pl.pallas_call(
        _gla_out_body,
        grid=(t // tm,),
        in_specs=[row(GLA_DV), row(GLA_DV), row(GLA_DV), _resident((1, GLA_DV)), row(d), _resident((GLA_DV, d))],
        out_specs=row(d),
        out_shape=jax.ShapeDtypeStruct((t, d), F32),
        compiler_params=_params("parallel"),
        name="gla_out",
    )(o_f.reshape(t, GLA_DV), o_b.reshape(t, GLA_DV), r, norm_g, h2, w_o)


def _gla_mixer(h, g, w_in, w_a1, w_a2, b_a, norm_g, w_o):
    b, s, d = h.shape
    rank = w_a1.shape[-1]
    wa1 = jnp.concatenate([w_a1[0], w_a1[1]], axis=1).astype(BF16)
    zeros = jnp.zeros((rank, GLA_DK), F32)
    wa2 = jnp.concatenate([jnp.concatenate([w_a2[0], zeros], axis=1),
                           jnp.concatenate([zeros, w_a2[1]], axis=1)], axis=0).astype(BF16)
    ba = jnp.concatenate([b_a[0], b_a[1]]).reshape(1, 2 * GLA_DK)
    h2 = h.reshape(b * s, d)
    q, k, v, r, la = _gla_proj(h2, g, w_in.astype(BF16), wa1, wa2, ba)
    o_f, o_b = _gla_scan(q, k, v, la, b, s)
    return _gla_out(o_f, o_b, r, norm_g.reshape(1, GLA_DV), h2, w_o.astype(BF16)).reshape(b, s, d)


def kernel(x, norm_g, final_norm_g, ffn_w1, ffn_w3, ffn_w2, attn_w_qkv, attn_w_o, conv_w_pw1, conv_b_pw1,
           conv_w_dw, conv_b_dw, conv_ln_g, conv_ln_b, conv_w_pw2, conv_b_pw2, fnet_w, fnet_b, gla_w_in,
           gla_w_a1, gla_w_a2, gla_b_a, gla_norm_g, gla_w_o):
    b, s, d = x.shape
    depth = norm_g.shape[0]
    n_mixers = 4
    gf = final_norm_g.reshape(1, d)
    h = x

    def ffn(h, i, half, final=False):
        out = _ffn(h.reshape(b * s, d), norm_g[i, 2 * half].reshape(1, d), ffn_w1[i, half].astype(BF16),
                   ffn_w3[i, half].astype(BF16), ffn_w2[i, half].astype(BF16), gf, final=final)
        return out.reshape(b, s, d)

    for i in range(depth):
        m, j = i % n_mixers, i // n_mixers
        h = ffn(h, i, 0)
        g = norm_g[i, 1].reshape(1, d)
        if m == 0:
            h = _attention_mixer(h, g, attn_w_qkv[j], attn_w_o[j])
        elif m == 1:
            h = _conv_mixer(h, g, conv_w_pw1[j], conv_b_pw1[j], conv_w_dw[j], conv_b_dw[j], conv_ln_g[j],
                            conv_ln_b[j], conv_w_pw2[j], conv_b_pw2[j])
        elif m == 2:
            h = _fourier_mixer(h, g, fnet_w[j], fnet_b[j])
        else:
            h = _gla_mixer(h, g, gla_w_in[j], gla_w_a1[j], gla_w_a2[j], gla_b_a[j], gla_norm_g[j], gla_w_o[j])
        h = ffn(h, i, 1, final=(i == depth - 1))
    return h
```

```python
import functools
import math

import numpy as np
import jax
import jax.numpy as jnp
from jax import lax
from jax.experimental import pallas as pl
from jax.experimental.pallas import tpu as pltpu

F32 = jnp.float32
BF16 = jnp.bfloat16
EPS = 1e-6
MASK_VALUE = -1e30

LANES = 128
SUBLANES = 8
VMEM_LIMIT_BYTES = 56 * 1024 * 1024

DIL_PATTERNS = ((128, 1), (512, 4), (2048, 16))
HEADS_PER_GROUP = 8
HEAD_DIM_A = 64
ATTN_WIDTH = HEADS_PER_GROUP * HEAD_DIM_A
BAND_RADIUS = 64
CONV_WIDTH = 31
CONV_HALO = 16
FOURIER_GROUP_DIM = 256
GLA_HEADS = 4
GLA_HEAD_K = 128
GLA_HEAD_V = 256
GLA_DK = GLA_HEADS * GLA_HEAD_K
GLA_DV = GLA_HEADS * GLA_HEAD_V
GLA_TAU = 16.0
GLA_CHUNK = 64

NT_DIMS = (((1,), (1,)), ((), ()))
TN_DIMS = (((0,), (0,)), ((), ()))


def _params(*semantics):
    return pltpu.CompilerParams(dimension_semantics=semantics, vmem_limit_bytes=VMEM_LIMIT_BYTES)


def _resident(shape):
    nd = len(shape)
    return pl.BlockSpec(shape, lambda *_: (0,) * nd, pipeline_mode=pl.Buffered(1))


def _rms(x, g):
    ms = jnp.mean(x * x, axis=-1, keepdims=True)
    return x * lax.rsqrt(ms + EPS) * g


def _silu(x):
    return x * jax.nn.sigmoid(x)


def _dot(a, b):
    return jnp.dot(a, b, preferred_element_type=F32)


def _ffn_body(x_ref, g_ref, w1_ref, w3_ref, w2_ref, gf_ref, o_ref, gate_ref, *, ff_chunk, final):
    x = x_ref[...]
    hn = _rms(x, g_ref[...]).astype(BF16)
    d_ff = w1_ref.shape[1]
    for c0 in range(0, d_ff, ff_chunk):
        sl = slice(c0, c0 + ff_chunk)
        a = _dot(hn, w1_ref[:, sl])
        b = _dot(hn, w3_ref[:, sl])
        gate_ref[:, sl] = (_silu(a) * b).astype(BF16)
    y = x + 0.5 * _dot(gate_ref[...], w2_ref[...])
    if final:
        y = _rms(y, gf_ref[...])
    o_ref[...] = y


def _ffn(h2, g, w1, w3, w2, gf, *, final, tm=512, ff_chunk=256):
    t, d = h2.shape
    d_ff = w1.shape[1]
    row = pl.BlockSpec((tm, d), lambda i: (i, 0))
    return pl.pallas_call(
        functools.partial(_ffn_body, ff_chunk=ff_chunk, final=final),
        grid=(t // tm,),
        in_specs=[row, _resident((1, d)), _resident((d, d_ff)), _resident((d, d_ff)),
                  _resident((d_ff, d)), _resident((1, d))],
        out_specs=row,
        out_shape=jax.ShapeDtypeStruct((t, d), F32),
        scratch_shapes=[pltpu.VMEM((tm, d_ff), BF16)],
        compiler_params=_params("parallel"),
        name="ffn_final" if final else "ffn",
    )(h2, g, w1, w3, w2, gf)


def _attn_proj_body(x_ref, g_ref, w_ref, o1_ref, o4_ref, o16_ref, y_ref, *, tm):
    u = _rms(x_ref[...], g_ref[...]).astype(BF16)
    n = 3 * ATTN_WIDTH
    o_refs = (o1_ref, o4_ref, o16_ref)

    def project(grp):
        return _dot(u, w_ref[:, grp * n:(grp + 1) * n])

    def emit(grp, y):
        o_ref, dil = o_refs[grp], DIL_PATTERNS[grp][1]
        if dil == 1:
            o_ref[0] = y.astype(BF16)
            return
        for c in range(n // LANES):
            y_ref[c] = y[:, c * LANES:(c + 1) * LANES]
        for r in range(dil):
            for c in range(n // LANES):
                o_ref[r, :, c * LANES:(c + 1) * LANES] = y_ref[c, pl.ds(r, tm // dil, stride=dil), :].astype(BF16)

    y = project(0)
    for grp in range(len(DIL_PATTERNS)):
        y_next = project(grp + 1) if grp + 1 < len(DIL_PATTERNS) else None
        emit(grp, y)
        y = y_next


def _attn_proj(h, g, w, tm=512):
    b, s, d = h.shape
    n = 3 * ATTN_WIDTH
    out_specs, out_shape = [], []
    for _, dil in DIL_PATTERNS:
        out_specs.append(pl.BlockSpec((None, dil, tm // dil, n), lambda bi, i: (bi, 0, i, 0)))
        out_shape.append(jax.ShapeDtypeStruct((b, dil, s // dil, n), BF16))
    return pl.pallas_call(
        functools.partial(_attn_proj_body, tm=tm),
        grid=(b, s // tm),
        in_specs=[pl.BlockSpec((None, tm, d), lambda bi, i: (bi, i, 0)), _resident((1, d)),
                  _resident((d, len(DIL_PATTERNS) * n))],
        out_specs=out_specs,
        out_shape=out_shape,
        scratch_shapes=[pltpu.VMEM((n // LANES, tm, LANES), F32)],
        compiler_params=_params("parallel", "parallel"),
        name="attn_proj",
    )(h, g, w)


def _band_body(q_ref, kp_ref, km_ref, kn_ref, vp_ref, vm_ref, vn_ref, o_ref, lse_ref, *, tq, sub, slopes):
    i = pl.program_id(2)
    rad = BAND_RADIUS
    nk = 3 * rad
    n_sb = tq // rad
    r = lax.broadcasted_iota(jnp.int32, (rad, nk), 0)
    c = lax.broadcasted_iota(jnp.int32, (rad, nk), 1)
    dist = jnp.abs(c - rad - r)
    absrel = dist.astype(F32)
    out_of_band = jnp.where(dist <= rad, 0.0, -MASK_VALUE)
    col = lax.broadcasted_iota(jnp.int32, (1, nk), 1)
    is_first = (i == 0).astype(F32)
    is_last = ((i + 1) * tq >= sub).astype(F32)
    before_start = jnp.where(col < rad, -MASK_VALUE, 0.0) * is_first
    after_end = jnp.where(col >= 2 * rad, -MASK_VALUE, 0.0) * is_last
    lane = lax.broadcasted_iota(jnp.int32, (rad, LANES), 1)
    low = lane < HEAD_DIM_A
    lowf = low.astype(F32)
    n_pairs = HEADS_PER_GROUP // 2

    def scores(p):
        sl = slice(LANES * p, LANES * (p + 1))
        kc = jnp.concatenate([kp_ref[:, sl], km_ref[:, sl], kn_ref[:, sl]], axis=0)
        pen = jnp.concatenate([slopes[2 * p + hh] * absrel + out_of_band for hh in range(2)], axis=0)
        out = []
        for sb in range(n_sb):
            q2 = q_ref[sb * rad:(sb + 1) * rad, sl].astype(F32)
            qm = jnp.concatenate([q2 * lowf, q2 * (1.0 - lowf)], axis=0).astype(BF16)
            sc = lax.dot_general(qm, kc[sb * rad:sb * rad + nk], NT_DIMS, preferred_element_type=F32) - pen
            if sb == 0:
                sc = sc - before_start
            if sb == n_sb - 1:
                sc = sc - after_end
            out.append(sc)
        return out

    def finish(p, scs):
        sl = slice(LANES * p, LANES * (p + 1))
        vc = jnp.concatenate([vp_ref[:, sl], vm_ref[:, sl], vn_ref[:, sl]], axis=0)
        for sb, sc in enumerate(scs):
            rows = slice(sb * rad, (sb + 1) * rad)
            m = jnp.max(sc, axis=-1, keepdims=True)
            e = jnp.exp(sc - m)
            l = jnp.sum(e, axis=-1, keepdims=True)
            o = _dot(e.astype(BF16), vc[sb * rad:sb * rad + nk]) / l
            lse = m + jnp.log(l)
            o_ref[rows, sl] = jnp.where(low, o[:rad], o[rad:]).astype(BF16)
            rest = 0.0 if p == 0 else lse_ref[rows, :]
            lse_ref[rows, :] = jnp.where(lane == 2 * p, lse[:rad], jnp.where(lane == 2 * p + 1, lse[rad:], rest))

    pending = scores(0)
    for p in range(1, n_pairs):
        upcoming = scores(p)
        finish(p - 1, pending)
        pending = upcoming
    finish(n_pairs - 1, pending)


def _band_attention(qkv, group, dil, tq=512):
    b, _, sub, _ = qkv.shape
    w = ATTN_WIDTH
    tq = min(tq, sub)
    hb = tq // BAND_RADIUS
    n_halo = sub // BAND_RADIUS
    n_heads = len(DIL_PATTERNS) * HEADS_PER_GROUP
    slopes = tuple(float(2.0 ** (-8.0 * (group * HEADS_PER_GROUP + hd + 1) / n_heads)) * dil
                   for hd in range(HEADS_PER_GROUP))

    def main(col):
        return pl.BlockSpec((None, None, tq, w), lambda bi, r, i: (bi, r, i, col))

    def prev(col):
        return pl.BlockSpec((None, None, BAND_RADIUS, w),
                            lambda bi, r, i: (bi, r, jnp.maximum(i * hb - 1, 0), col))

    def nxt(col):
        return pl.BlockSpec((None, None, BAND_RADIUS, w),
                            lambda bi, r, i: (bi, r, jnp.minimum((i + 1) * hb, n_halo - 1), col))

    return pl.pallas_call(
        functools.partial(_band_body, tq=tq, sub=sub, slopes=slopes),
        grid=(b, dil, sub // tq),
        in_specs=[main(0), prev(1), main(1), nxt(1), prev(2), main(2), nxt(2)],
        out_specs=[main(0), pl.BlockSpec((None, None, tq, LANES), lambda bi, r, i: (bi, r, i, 0))],
        out_shape=[jax.ShapeDtypeStruct((b, dil, sub, w), BF16), jax.ShapeDtypeStruct((b, dil, sub, LANES), F32)],
        compiler_params=_params("parallel", "parallel", "parallel"),
        name=f"band_attn_d{dil}",
    )(qkv, qkv, qkv, qkv, qkv, qkv, qkv)


def _attn_out_body(o1, o4, o16, l1, l4, l16, h_ref, w_ref, out_ref, os_ref, ls_ref, *, tm):
    n_lane_groups = ATTN_WIDTH // LANES
    for grp, (o_ref, l_ref, (_, dil)) in enumerate(zip((o1, o4, o16), (l1, l4, l16), DIL_PATTERNS)):
        for r in range(dil):
            rows = pl.ds(r, tm // dil, stride=dil) if dil > 1 else slice(None)
            ls_ref[grp, rows, :] = l_ref[r]
            for c in range(n_lane_groups):
                os_ref[grp, c, rows, :] = o_ref[r, :, c * LANES:(c + 1) * LANES].astype(F32)
    ls = [ls_ref[grp] for grp in range(len(DIL_PATTERNS))]
    m = jnp.maximum(jnp.maximum(ls[0], ls[1]), ls[2])
    es = [jnp.exp(l - m) for l in ls]
    inv = 1.0 / (es[0] + es[1] + es[2])
    row = lax.broadcasted_iota(jnp.int32, (LANES, ATTN_WIDTH), 0)
    col = lax.broadcasted_iota(jnp.int32, (LANES, ATTN_WIDTH), 1)
    spread = jnp.where(col // HEAD_DIM_A == row, 1.0, 0.0).astype(BF16)
    o = 0.0
    for grp, e in enumerate(es):
        wgt = e * inv
        hi = wgt.astype(BF16)
        lo = (wgt - hi.astype(F32)).astype(BF16)
        full = _dot(hi, spread) + _dot(lo, spread)
        o = o + full * jnp.concatenate([os_ref[grp, c] for c in range(n_lane_groups)], axis=1)
    out_ref[...] = h_ref[...] + _dot(o.astype(BF16), w_ref[...])


def _attn_out(os_, lses, h, w_o, tm=512):
    b, s, d = h.shape
    w = ATTN_WIDTH
    specs = [pl.BlockSpec((None, dil, tm // dil, w), lambda bi, i: (bi, 0, i, 0)) for _, dil in DIL_PATTERNS]
    lspecs = [pl.BlockSpec((None, dil, tm // dil, LANES), lambda bi, i: (bi, 0, i, 0)) for _, dil in DIL_PATTERNS]
    row = pl.BlockSpec((None, tm, d), lambda bi, i: (bi, i, 0))
    n_grp = len(DIL_PATTERNS)
    return pl.pallas_call(
        functools.partial(_attn_out_body, tm=tm),
        grid=(b, s // tm),
        in_specs=specs + lspecs + [row, _resident((w, d))],
        out_specs=row,
        out_shape=jax.ShapeDtypeStruct((b, s, d), F32),
        scratch_shapes=[pltpu.VMEM((n_grp, w // LANES, tm, LANES), F32), pltpu.VMEM((n_grp, tm, LANES), F32)],
        compiler_params=_params("parallel", "parallel"),
        name="attn_out",
    )(*os_, *lses, h, w_o)


def _attention_mixer(h, g, w_qkv, w_o):
    w = ATTN_WIDTH
    col = jnp.arange(w_qkv.shape[1]) % (3 * w)
    wq = (w_qkv * jnp.where(col < w, HEAD_DIM_A ** -0.5, 1.0)).astype(BF16)
    qkvs = _attn_proj(h, g, wq)
    os_, lses = [], []
    for grp, (_, dil) in enumerate(DIL_PATTERNS):
        o, lse = _band_attention(qkvs[grp], grp, dil)
        os_.append(o)
        lses.append(lse)
    return _attn_out(os_, lses, h, w_o.astype(BF16))


def _conv_body(xp_ref, xm_ref, xn_ref, g_ref, w1_ref, b1_ref, wdw_ref, bdw_ref, lng_ref, lnb_ref, w2_ref, b2_ref,
               o_ref, xext_ref, zext_ref, zs_ref, cv_ref, *, tm, n_tiles, row_chunk, lane_chunk):
    i = pl.program_id(1)
    d = xm_ref.shape[1]
    halo = CONV_HALO
    ext = tm + 2 * halo
    xext_ref[0:halo, :] = xp_ref[...]
    xext_ref[halo:halo + tm, :] = xm_ref[...]
    xext_ref[halo + tm:ext, :] = xn_ref[...]
    u = _rms(xext_ref[...], g_ref[...]).astype(BF16)
    z = _dot(u, w1_ref[...]) + b1_ref[...]
    zext_ref[...] = z[:, :d] * jax.nn.sigmoid(z[:, d:])
    zext_ref[0:halo, :] = jnp.where(i > 0, zext_ref[0:halo, :], 0.0)
    zext_ref[halo + tm:ext, :] = jnp.where(i < n_tiles - 1, zext_ref[halo + tm:ext, :], 0.0)
    span = ext - SUBLANES
    for r in range(1, SUBLANES):
        for c0 in range(0, d, LANES):
            zs_ref[r - 1, :, c0:c0 + LANES] = zext_ref[r:r + span, c0:c0 + LANES]
    first = halo - CONV_WIDTH // 2
    for c0 in range(0, d, lane_chunk):
        cs = slice(c0, c0 + lane_chunk)
        for r0 in range(0, tm, row_chunk):
            acc = jnp.zeros((row_chunk, lane_chunk), F32)
            for k in range(CONV_WIDTH):
                q, r = divmod(first + k, SUBLANES)
                rows = slice(r0 + q * SUBLANES, r0 + q * SUBLANES + row_chunk)
                tap = zext_ref[rows, cs] if r == 0 else zs_ref[r - 1, rows, cs]
                acc = acc + wdw_ref[k:k + 1, cs] * tap
            cv_ref[r0:r0 + row_chunk, cs] = acc + bdw_ref[:, cs]
    z = cv_ref[...]
    mu = jnp.mean(z, axis=-1, keepdims=True)
    zc = z - mu
    var = jnp.mean(zc * zc, axis=-1, keepdims=True)
    zn = zc * lax.rsqrt(var + EPS) * lng_ref[...] + lnb_ref[...]
    y = _silu(zn).astype(BF16)
    o_ref[...] = xm_ref[...] + _dot(y, w2_ref[...]) + b2_ref[...]


def _conv_mixer(h, g, w_pw1, b_pw1, w_dw, b_dw, ln_g, ln_b, w_pw2, b_pw2, tm=512):
    b, s, d = h.shape
    n_tiles = s // tm
    hb = tm // CONV_HALO
    n_halo = s // CONV_HALO
    ext = tm + 2 * CONV_HALO
    main = pl.BlockSpec((None, tm, d), lambda bi, i: (bi, i, 0))
    prev = pl.BlockSpec((None, CONV_HALO, d), lambda bi, i: (bi, jnp.maximum(i * hb - 1, 0), 0))
    nxt = pl.BlockSpec((None, CONV_HALO, d), lambda bi, i: (bi, jnp.minimum((i + 1) * hb, n_halo - 1), 0))
    vec = _resident((1, d))
    return pl.pallas_call(
        functools.partial(_conv_body, tm=tm, n_tiles=n_tiles, row_chunk=128, lane_chunk=256),
        grid=(b, n_tiles),
        in_specs=[prev, main, nxt, vec, _resident((d, 2 * d)), _resident((1, 2 * d)), _resident((CONV_WIDTH, d)),
                  vec, vec, vec, _resident((d, d)), vec],
        out_specs=main,
        out_shape=jax.ShapeDtypeStruct((b, s, d), F32),
        scratch_shapes=[pltpu.VMEM((ext, d), F32), pltpu.VMEM((ext, d), F32),
                        pltpu.VMEM((SUBLANES - 1, ext - SUBLANES, d), F32), pltpu.VMEM((tm, d), F32)],
        compiler_params=_params("parallel", "parallel"),
        name="conv_module",
    )(h, h, h, g, w_pw1.astype(BF16), b_pw1.reshape(1, 2 * d), w_dw, b_dw.reshape(1, d), ln_g.reshape(1, d),
      ln_b.reshape(1, d), w_pw2.astype(BF16), b_pw2.reshape(1, d))


def _pack_complex(re, im):
    hi = lax.bitcast_convert_type(re.astype(BF16).astype(F32), jnp.uint32)
    lo = lax.bitcast_convert_type(im.astype(BF16).astype(F32), jnp.uint32)
    return hi | (lo >> 16)


def _unpack_complex(word):
    re = lax.bitcast_convert_type(word & jnp.uint32(0xFFFF0000), F32)
    im = lax.bitcast_convert_type(word << 16, F32)
    return re.astype(BF16), im.astype(BF16)


def _fft1_body(x_ref, g_ref, w1_ref, ct_ref, st_ref, y_ref):
    n1, n2s, d = x_ref.shape
    rows = n1 * n2s
    u = _rms(x_ref[...].reshape(rows, d), g_ref[...]).astype(BF16)
    y = _dot(w1_ref[...], u)
    yr, ys = y[:rows], y[rows:]
    ct = jnp.concatenate([ct_ref[...].reshape(rows, LANES)] * (d // LANES), axis=1)
    st = jnp.concatenate([st_ref[...].reshape(rows, LANES)] * (d // LANES), axis=1)
    y_ref[...] = _pack_complex(yr * ct - ys * st, -(yr * st + ys * ct)).reshape(n1, n2s, d)


def _fft2_body(y_ref, h_ref, m2_ref, cc_ref, sc_ref, wf_ref, bf_ref, o_ref, zr_ref, zi_ref, mix_ref, res_ref, *,
               out_rows):
    k1s, n2, d = y_ref.shape
    for jj in range(k1s):
        re, im = _unpack_complex(y_ref[jj])
        z = _dot(m2_ref[...], jnp.concatenate([re, im], axis=0))
        zr_ref[jj * n2:(jj + 1) * n2, :] = z[:n2].astype(BF16)
        zi_ref[jj * n2:(jj + 1) * n2, :] = z[n2:].astype(BF16)
    gd = FOURIER_GROUP_DIM
    for c0 in range(0, d, gd):
        cs = slice(c0, c0 + gd)
        mix_ref[:, cs] = (_dot(zr_ref[:, cs], cc_ref[...]) + _dot(zi_ref[:, cs], sc_ref[...])).astype(BF16)
    for r0 in range(0, k1s * n2, out_rows):
        res_ref[r0:r0 + out_rows, :] = _dot(mix_ref[r0:r0 + out_rows, :], wf_ref[...]) + bf_ref[...]
    o_ref[...] = h_ref[...] + jnp.swapaxes(res_ref[...].reshape(k1s, n2, d), 0, 1)


def _dft_tables(s, n2_per_step):
    n2 = 128
    n1 = s // n2
    def cs(n):
        ang = 2.0 * np.pi * (np.outer(np.arange(n), np.arange(n)) % n) / n
        return np.cos(ang), np.sin(ang)
    c1, s1 = cs(n1)
    c2, s2 = cs(n2)
    cc, sc = cs(FOURIER_GROUP_DIM)
    eye = np.eye(n2_per_step)
    scale = 1.0 / math.sqrt(s * FOURIER_GROUP_DIM)
    ang_t = 2.0 * np.pi * (np.outer(np.arange(n1), np.arange(n2)) % s) / s
    return dict(
        n1=n1, n2=n2,
        w1=jnp.asarray(np.concatenate([np.kron(c1, eye), np.kron(s1, eye)], axis=0), F32).astype(BF16),
        m2=jnp.asarray(np.block([[c2, s2], [-s2, c2]]), F32).astype(BF16),
        cc=jnp.asarray(cc * scale, F32).astype(BF16),
        sc=jnp.asarray(sc * scale, F32).astype(BF16),
        ct=jnp.asarray(np.cos(ang_t), F32), st=jnp.asarray(np.sin(ang_t), F32))


def _fourier_mixer(h, g, w_f, b_f, n2_per_step=SUBLANES, k1_per_step=SUBLANES, out_rows=256):
    b, s, d = h.shape
    tb = _dft_tables(s, n2_per_step)
    n1, n2 = tb["n1"], tb["n2"]
    ct = jnp.broadcast_to(tb["ct"][:, :, None], (n1, n2, LANES))
    st = jnp.broadcast_to(tb["st"][:, :, None], (n1, n2, LANES))
    rows1 = n1 * n2_per_step
    xblk = pl.BlockSpec((None, n1, n2_per_step, d), lambda bi, j: (bi, 0, j, 0))
    tw = pl.BlockSpec((n1, n2_per_step, LANES), lambda bi, j: (0, j, 0))
    y = pl.pallas_call(
        _fft1_body,
        grid=(b, n2 // n2_per_step),
        in_specs=[xblk, _resident((1, d)), _resident((2 * rows1, rows1)), tw, tw],
        out_specs=xblk,
        out_shape=jax.ShapeDtypeStruct((b, n1, n2, d), jnp.uint32),
        compiler_params=_params("parallel", "parallel"),
        name="fft_stage1",
    )(h.reshape(b, n1, n2, d), g, tb["w1"], ct, st)
    gd = FOURIER_GROUP_DIM
    rows2 = k1_per_step * n2
    oblk = pl.BlockSpec((None, n2, k1_per_step, d), lambda bi, k: (bi, 0, k, 0))
    out = pl.pallas_call(
        functools.partial(_fft2_body, out_rows=out_rows),
        grid=(b, n1 // k1_per_step),
        in_specs=[pl.BlockSpec((None, k1_per_step, n2, d), lambda bi, k: (bi, k, 0, 0)), oblk,
                  _resident((2 * n2, 2 * n2)), _resident((gd, gd)), _resident((gd, gd)),
                  _resident((d, d)), _resident((1, d))],
        out_specs=oblk,
        out_shape=jax.ShapeDtypeStruct((b, n2, n1, d), F32),
        scratch_shapes=[pltpu.VMEM((rows2, d), BF16)] * 3 + [pltpu.VMEM((rows2, d), F32)],
        compiler_params=_params("parallel", "parallel"),
        name="fft_stage2",
    )(y, h.reshape(b, n2, n1, d), tb["m2"], tb["cc"], tb["sc"], w_f.astype(BF16), b_f.reshape(1, d))
    return out.reshape(b, s, d)


def _gla_proj_body(x_ref, g_ref, win_ref, wa1_ref, wa2_ref, ba_ref, q_ref, k_ref, v_ref, r_ref, la_ref):
    u = _rms(x_ref[...], g_ref[...]).astype(BF16)
    proj = _dot(u, win_ref[...])
    dk = q_ref.shape[1]
    dv = v_ref.shape[1]
    q_ref[...] = (proj[:, :dk] * (GLA_HEAD_K ** -0.5)).astype(BF16)
    k_ref[...] = proj[:, dk:2 * dk].astype(BF16)
    v_ref[...] = proj[:, 2 * dk:2 * dk + dv].astype(BF16)
    r_ref[...] = proj[:, 2 * dk + dv:].astype(BF16)
    low = _dot(u, wa1_ref[...]).astype(BF16)
    z = _dot(low, wa2_ref[...]) + ba_ref[...]
    log_sig = jnp.minimum(z, 0.0) - jnp.log(1.0 + jnp.exp(-jnp.abs(z)))
    la_ref[...] = log_sig / GLA_TAU


def _gla_proj(h2, g, w_in, wa1, wa2, ba, tm=512):
    t, d = h2.shape
    row = lambda n: pl.BlockSpec((tm, n), lambda i: (i, 0))
    rank2 = wa1.shape[1]
    return pl.pallas_call(
        _gla_proj_body,
        grid=(t // tm,),
        in_specs=[row(d), _resident((1, d)), _resident(w_in.shape), _resident((d, rank2)),
                  _resident((rank2, 2 * GLA_DK)), _resident((1, 2 * GLA_DK))],
        out_specs=[row(GLA_DK), row(GLA_DK), row(GLA_DV), row(GLA_DV), row(2 * GLA_DK)],
        out_shape=[jax.ShapeDtypeStruct((t, GLA_DK), BF16), jax.ShapeDtypeStruct((t, GLA_DK), BF16),
                   jax.ShapeDtypeStruct((t, GLA_DV), BF16), jax.ShapeDtypeStruct((t, GLA_DV), BF16),
                   jax.ShapeDtypeStruct((t, 2 * GLA_DK), F32)],
        compiler_params=_params("parallel"),
        name="gla_proj",
    )(h2, g, w_in, wa1, wa2, ba)


def _gla_prepare(q_ref, k_ref, la_ref, *, backward, n_chunks):
    c = GLA_CHUNK
    rows = n_chunks * c
    ri = lax.broadcasted_iota(jnp.int32, (rows, rows), 0)
    ci = lax.broadcasted_iota(jnp.int32, (rows, rows), 1)
    same_chunk = (ri // c) == (ci // c)
    if backward:
        tri = same_chunk & (ci >= ri)
        keep = same_chunk & (ci > ri)
    else:
        tri = same_chunk & (ci <= ri)
        keep = tri
    tri = jnp.where(tri, 1.0, 0.0).astype(BF16)
    la = la_ref[...]
    la_hi = la.astype(BF16)
    la_lo = (la - la_hi.astype(F32)).astype(BF16)
    cum = _dot(tri, la_hi) + _dot(tri, la_lo)
    ref_row = c // 2 if backward else c // 2 - 1
    last_row = 0 if backward else c - 1
    lasts = [cum[a * c + last_row:a * c + last_row + 1] for a in range(n_chunks)]
    ref = jnp.concatenate([jnp.broadcast_to(cum[a * c + ref_row:a * c + ref_row + 1], (c, cum.shape[1]))
                           for a in range(n_chunks)], axis=0)
    last = jnp.concatenate([jnp.broadcast_to(l, (c, cum.shape[1])) for l in lasts], axis=0)
    q = q_ref[...].astype(F32)
    k = k_ref[...].astype(F32)
    return dict(
        keep=keep, lasts=lasts, order=range(n_chunks - 1, -1, -1) if backward else range(n_chunks),
        q_in=(q * jnp.exp(cum - ref)).astype(BF16), k_in=(k * jnp.exp(ref - cum)).astype(BF16),
        q_x=(q * jnp.exp(cum)).astype(BF16), k_x=(k * jnp.exp(last - cum)).astype(BF16))


def _gla_scan_body(qf, kf, vf, laf, qb, kb, vb, lab, of_ref, ob_ref, stf_ref, stb_ref, accf_ref, accb_ref, *,
                   chunks_per_step):
    @pl.when(pl.program_id(1) == 0)
    def _():
        stf_ref[...] = jnp.zeros_like(stf_ref)
        stb_ref[...] = jnp.zeros_like(stb_ref)
    c = GLA_CHUNK
    heads = [(slice(GLA_HEAD_K * hd, GLA_HEAD_K * (hd + 1)), slice(GLA_HEAD_V * hd, GLA_HEAD_V * (hd + 1)))
             for hd in range(GLA_HEADS)]
    dirs = [(_gla_prepare(qf, kf, laf, backward=False, n_chunks=chunks_per_step), vf, of_ref, stf_ref, accf_ref),
            (_gla_prepare(qb, kb, lab, backward=True, n_chunks=chunks_per_step), vb, ob_ref, stb_ref, accb_ref)]
    scores = [[lax.dot_general(p["q_in"][:, ks], p["k_in"][:, ks], NT_DIMS, preferred_element_type=F32)
               for ks, _ in heads] for p, *_ in dirs]
    for (p, v_ref, _, _, acc_ref), scs in zip(dirs, scores):
        for (_, vs), sc in zip(heads, scs):
            acc_ref[:, vs] = _dot(jnp.where(p["keep"], sc, 0.0).astype(BF16), v_ref[:, vs])
    states = [[st_ref[hd] for hd in range(GLA_HEADS)] for _, _, _, st_ref, _ in dirs]
    for step in range(chunks_per_step):
        for di, (p, v_ref, o_ref, _, acc_ref) in enumerate(dirs):
            a = p["order"][step]
            rs = slice(a * c, (a + 1) * c)
            for hd, (ks, vs) in enumerate(heads):
                state_t = states[di][hd]
                inter = lax.dot_general(p["q_x"][rs, ks], state_t.astype(BF16), NT_DIMS, preferred_element_type=F32)
                o_ref[rs, vs] = (acc_ref[rs, vs] + inter).astype(o_ref.dtype)
                states[di][hd] = state_t * jnp.exp(p["lasts"][a][:, ks]) + lax.dot_general(
                    v_ref[rs, vs], p["k_x"][rs, ks], TN_DIMS, preferred_element_type=F32)
    for di, (_, _, _, st_ref, _) in enumerate(dirs):
        for hd in range(GLA_HEADS):
            st_ref[hd] = states[di][hd]


def _gla_scan(q, k, v, la, b, s, chunks_per_step=4):
    rows = chunks_per_step * GLA_CHUNK
    nb = s // rows
    fwd = lambda n, col=0: pl.BlockSpec((None, rows, n), lambda bi, i: (bi, i, col))
    bwd = lambda n, col=0: pl.BlockSpec((None, rows, n), lambda bi, i: (bi, nb - 1 - i, col))
    q3, k3, v3 = (a.reshape(b, s, a.shape[-1]) for a in (q, k, v))
    la3 = la.reshape(b, s, 2 * GLA_DK)
    state = pltpu.VMEM((GLA_HEADS, GLA_HEAD_V, GLA_HEAD_K), F32)
    return pl.pallas_call(
        functools.partial(_gla_scan_body, chunks_per_step=chunks_per_step),
        grid=(b, nb),
        in_specs=[fwd(GLA_DK), fwd(GLA_DK), fwd(GLA_DV), fwd(GLA_DK, 0),
                  bwd(GLA_DK), bwd(GLA_DK), bwd(GLA_DV), bwd(GLA_DK, 1)],
        out_specs=[fwd(GLA_DV), bwd(GLA_DV)],
        out_shape=[jax.ShapeDtypeStruct((b, s, GLA_DV), BF16)] * 2,
        scratch_shapes=[state, state, pltpu.VMEM((rows, GLA_DV), F32), pltpu.VMEM((rows, GLA_DV), F32)],
        compiler_params=_params("parallel", "arbitrary"),
        name="gla_scan",
    )(q3, k3, v3, la3, q3, k3, v3, la3)


def _gla_out_body(of_ref, ob_ref, r_ref, ng_ref, h_ref, w_ref, out_ref):
    o = of_ref[...].astype(F32) + ob_ref[...].astype(F32)
    parts = []
    for hd in range(GLA_HEADS):
        oh = o[:, GLA_HEAD_V * hd:GLA_HEAD_V * (hd + 1)]
        parts.append(oh * lax.rsqrt(jnp.mean(oh * oh, axis=-1, keepdims=True) + EPS))
    o = jnp.concatenate(parts, axis=1) * ng_ref[...] * _silu(r_ref[...].astype(F32))
    out_ref[...] = h_ref[...] + _dot(o.astype(BF16), w_ref[...])


def _gla_out(o_f, o_b, r, norm_g, h2, w_o, tm=512):
    t, d = h2.shape
    row = lambda n: pl.BlockSpec((tm, n), lambda i: (i, 0))
    return pl.pallas_call(
        _gla_out_body,
        grid=(t // tm,),
        in_specs=[row(GLA_DV), row(GLA_DV), row(GLA_DV), _resident((1, GLA_DV)), row(d), _resident((GLA_DV, d))],
        out_specs=row(d),
        out_shape=jax.ShapeDtypeStruct((t, d), F32),
        compiler_params=_params("parallel"),
        name="gla_out",
    )(o_f.reshape(t, GLA_DV), o_b.reshape(t, GLA_DV), r, norm_g, h2, w_o)


def _gla_mixer(h, g, w_in, w_a1, w_a2, b_a, norm_g, w_o):
    b, s, d = h.shape
    rank = w_a1.shape[-1]
    wa1 = jnp.concatenate([w_a1[0], w_a1[1]], axis=1).astype(BF16)
    zeros = jnp.zeros((rank, GLA_DK), F32)
    wa2 = jnp.concatenate([jnp.concatenate([w_a2[0], zeros], axis=1),
                           jnp.concatenate([zeros, w_a2[1]], axis=1)], axis=0).astype(BF16)
    ba = jnp.concatenate([b_a[0], b_a[1]]).reshape(1, 2 * GLA_DK)
    h2 = h.reshape(b * s, d)
    q, k, v, r, la = _gla_proj(h2, g, w_in.astype(BF16), wa1, wa2, ba)
    o_f, o_b = _gla_scan(q, k, v, la, b, s)
    return _gla_out(o_f, o_b, r, norm_g.reshape(1, GLA_DV), h2, w_o.astype(BF16)).reshape(b, s, d)


def kernel(x, norm_g, final_norm_g, ffn_w1, ffn_w3, ffn_w2, attn_w_qkv, attn_w_o, conv_w_pw1, conv_b_pw1,
           conv_w_dw, conv_b_dw, conv_ln_g, conv_ln_b, conv_w_pw2, conv_b_pw2, fnet_w, fnet_b, gla_w_in,
           gla_w_a1, gla_w_a2, gla_b_a, gla_norm_g, gla_w_o):
    b, s, d = x.shape
    depth = norm_g.shape[0]
    n_mixers = 4
    gf = final_norm_g.reshape(1, d)
    h = x

    def ffn(h, i, half, final=False):
        out = _ffn(h.reshape(b * s, d), norm_g[i, 2 * half].reshape(1, d), ffn_w1[i, half].astype(BF16),
                   ffn_w3[i, half].astype(BF16), ffn_w2[i, half].astype(BF16), gf, final=final)
        return out.reshape(b, s, d)

    for i in range(depth):
        m, j = i % n_mixers, i // n_mixers
        h = ffn(h, i, 0)
        g = norm_g[i, 1].reshape(1, d)
        if m == 0:
            h = _attention_mixer(h, g, attn_w_qkv[j], attn_w_o[j])
        elif m == 1:
            h = _conv_mixer(h, g, conv_w_pw1[j], conv_b_pw1[j], conv_w_dw[j], conv_b_dw[j], conv_ln_g[j],
                            conv_ln_b[j], conv_w_pw2[j], conv_b_pw2[j])
        elif m == 2:
            h = _fourier_mixer(h, g, fnet_w[j], fnet_b[j])
        else:
            h = _gla_mixer(h, g, gla_w_in[j], gla_w_a1[j], gla_w_a2[j], gla_b_a[j], gla_norm_g[j], gla_w_o[j])
        h = ffn(h, i, 1, final=(i == depth - 1))
    return h
```

```python
import functools
import math

import numpy as np
import jax
import jax.numpy as jnp
from jax import lax
from jax.experimental import pallas as pl
from jax.experimental.pallas import tpu as pltpu

F32 = jnp.float32
BF16 = jnp.bfloat16
EPS = 1e-6
MASK_VALUE = -1e30

LANES = 128
SUBLANES = 8
VMEM_LIMIT_BYTES = 56 * 1024 * 1024

DIL_PATTERNS = ((128, 1), (512, 4), (2048, 16))
HEADS_PER_GROUP = 8
HEAD_DIM_A = 64
ATTN_WIDTH = HEADS_PER_GROUP * HEAD_DIM_A
BAND_RADIUS = 64
CONV_WIDTH = 31
CONV_HALO = 16
FOURIER_GROUP_DIM = 256
GLA_HEADS = 4
GLA_HEAD_K = 128
GLA_HEAD_V = 256
GLA_DK = GLA_HEADS * GLA_HEAD_K
GLA_DV = GLA_HEADS * GLA_HEAD_V
GLA_TAU = 16.0
GLA_CHUNK = 64

NT_DIMS = (((1,), (1,)), ((), ()))
TN_DIMS = (((0,), (0,)), ((), ()))


def _params(*semantics):
    return pltpu.CompilerParams(dimension_semantics=semantics, vmem_limit_bytes=VMEM_LIMIT_BYTES)


def _resident(shape):
    nd = len(shape)
    return pl.BlockSpec(shape, lambda *_: (0,) * nd, pipeline_mode=pl.Buffered(1))


def _rms(x, g):
    ms = jnp.mean(x * x, axis=-1, keepdims=True)
    return x * lax.rsqrt(ms + EPS) * g


def _silu(x):
    return x * jax.nn.sigmoid(x)


def _dot(a, b):
    return jnp.dot(a, b, preferred_element_type=F32)


def _ffn_blocks(block_input, x_ref, g_ref, w1_ref, w3_ref, w2_ref, gf_ref, o_ref, gate_ref, *, ff_chunk, n_sub, final):
    tm, d = o_ref.shape
    d_ff = w1_ref.shape[1]
    rows = [slice(i * (tm // n_sub), (i + 1) * (tm // n_sub)) for i in range(n_sub)]
    stages = [block_input(i, rs) if block_input is not None else iter(()) for i, rs in enumerate(rows)]
    for _ in stages[0]:
        pass
    for i, rs in enumerate(rows):
        ahead = stages[i + 1] if i + 1 < n_sub else iter(())
        hn = _rms(x_ref[rs, :], g_ref[...]).astype(BF16)
        for c0 in range(0, d_ff, ff_chunk):
            sl = slice(c0, c0 + ff_chunk)
            a = _dot(hn, w1_ref[:, sl])
            b = _dot(hn, w3_ref[:, sl])
            gate_ref[rs, sl] = (_silu(a) * b).astype(BF16)
            next(ahead, None)
        for _ in ahead:
            pass
    for rs in rows:
        if final:
            y = x_ref[rs, :] + 0.5 * _dot(gate_ref[rs, :], w2_ref[...])
            o_ref[rs, :] = _rms(y, gf_ref[...])
        else:
            for c0 in range(0, d, ff_chunk):
                sl = slice(c0, c0 + ff_chunk)
                o_ref[rs, sl] = x_ref[rs, sl] + 0.5 * _dot(gate_ref[rs, :], w2_ref[:, sl])


def _ffn_body(x_ref, g_ref, w1_ref, w3_ref, w2_ref, gf_ref, o_ref, gate_ref, *, ff_chunk, n_sub, final):
    _ffn_blocks(None, x_ref, g_ref, w1_ref, w3_ref, w2_ref, gf_ref, o_ref, gate_ref,
                ff_chunk=ff_chunk, n_sub=n_sub, final=final)


def _mixer_ffn_body(*refs, n_in, block_input, ff_chunk, n_sub, final):
    ins = refs[:n_in]
    g_ref, w1_ref, w3_ref, w2_ref, gf_ref, o_ref, x_ref, gate_ref = refs[n_in:n_in + 8]
    scratch = refs[n_in + 8:]
    _ffn_blocks(lambda i, rs: block_input(i, rs, ins, scratch, x_ref), x_ref, g_ref, w1_ref, w3_ref, w2_ref,
                gf_ref, o_ref, gate_ref, ff_chunk=ff_chunk, n_sub=n_sub, final=final)


def _mixer_ffn(block_input, inputs, in_specs, scratch_shapes, ffn_params, *, grid, out_spec, out_shape, tm, final,
               name, ff_chunk=256, n_sub=2):
    g, w1, w3, w2, gf = ffn_params
    d, d_ff = w1.shape
    return pl.pallas_call(
        functools.partial(_mixer_ffn_body, n_in=len(inputs), block_input=block_input, ff_chunk=ff_chunk,
                          n_sub=n_sub, final=final),
        grid=grid,
        in_specs=list(in_specs) + [_resident((1, d)), _resident((d, d_ff)), _resident((d, d_ff)),
                                   _resident((d_ff, d)), _resident((1, d))],
        out_specs=out_spec,
        out_shape=out_shape,
        scratch_shapes=[pltpu.VMEM((tm, d), F32), pltpu.VMEM((tm, d_ff), BF16)] + list(scratch_shapes),
        compiler_params=_params(*(("parallel",) * len(grid))),
        name=name,
    )(*inputs, g, w1, w3, w2, gf)


def _ffn(h2, g, w1, w3, w2, gf, *, final, tm=1024, ff_chunk=256, n_sub=2):
    t, d = h2.shape
    d_ff = w1.shape[1]
    row = pl.BlockSpec((tm, d), lambda i: (i, 0))
    return pl.pallas_call(
        functools.partial(_ffn_body, ff_chunk=ff_chunk, n_sub=n_sub, final=final),
        grid=(t // tm,),
        in_specs=[row, _resident((1, d)), _resident((d, d_ff)), _resident((d, d_ff)),
                  _resident((d_ff, d)), _resident((1, d))],
        out_specs=row,
        out_shape=jax.ShapeDtypeStruct((t, d), F32),
        scratch_shapes=[pltpu.VMEM((tm, d_ff), BF16)],
        compiler_params=_params("parallel"),
        name="ffn_final" if final else "ffn",
    )(h2, g, w1, w3, w2, gf)


def _attn_proj_body(x_ref, g_ref, w_ref, o1_ref, o4_ref, o16_ref, y_ref, *, tm, n_sub):
    n = 3 * ATTN_WIDTH
    o_refs = (o1_ref, o4_ref, o16_ref)
    blk = tm // n_sub
    normed = {}

    def project(i, grp):
        if i not in normed:
            normed[i] = _rms(x_ref[i * blk:(i + 1) * blk, :], g_ref[...]).astype(BF16)
        return _dot(normed[i], w_ref[:, grp * n:(grp + 1) * n])

    def emit(i, grp, y):
        o_ref, dil = o_refs[grp], DIL_PATTERNS[grp][1]
        out_rows = slice(i * (blk // dil), (i + 1) * (blk // dil))
        if dil == 1:
            o_ref[0, out_rows, :] = y.astype(BF16)
            return
        for c in range(n // LANES):
            y_ref[c] = y[:, c * LANES:(c + 1) * LANES]
        for r in range(dil):
            for c in range(n // LANES):
                o_ref[r, out_rows, c * LANES:(c + 1) * LANES] = (
                    y_ref[c, pl.ds(r, blk // dil, stride=dil), :].astype(BF16))

    work = [(i, grp) for i in range(n_sub) for grp in range(len(DIL_PATTERNS))]
    y = project(*work[0])
    for j, item in enumerate(work):
        y_next = project(*work[j + 1]) if j + 1 < len(work) else None
        emit(*item, y)
        y = y_next


def _attn_proj(h, g, w, tm=512, n_sub=1):
    b, s, d = h.shape
    n = 3 * ATTN_WIDTH
    out_specs, out_shape = [], []
    for _, dil in DIL_PATTERNS:
        out_specs.append(pl.BlockSpec((None, dil, tm // dil, n), lambda bi, i: (bi, 0, i, 0)))
        out_shape.append(jax.ShapeDtypeStruct((b, dil, s // dil, n), BF16))
    return pl.pallas_call(
        functools.partial(_attn_proj_body, tm=tm, n_sub=n_sub),
        grid=(b, s // tm),
        in_specs=[pl.BlockSpec((None, tm, d), lambda bi, i: (bi, i, 0)), _resident((1, d)),
                  _resident((d, len(DIL_PATTERNS) * n))],
        out_specs=out_specs,
        out_shape=out_shape,
        scratch_shapes=[pltpu.VMEM((n // LANES, tm // n_sub, LANES), F32)],
        compiler_params=_params("parallel", "parallel"),
        name="attn_proj",
    )(h, g, w)


def _band_body(q_ref, kp_ref, km_ref, kn_ref, vp_ref, vm_ref, vn_ref, o_ref, lse_ref, *, tq, sub, slopes):
    i = pl.program_id(2)
    rad = BAND_RADIUS
    nk = 3 * rad
    n_sb = tq // rad
    r = lax.broadcasted_iota(jnp.int32, (rad, nk), 0)
    c = lax.broadcasted_iota(jnp.int32, (rad, nk), 1)
    dist = jnp.abs(c - rad - r)
    absrel = dist.astype(F32)
    out_of_band = jnp.where(dist <= rad, 0.0, -MASK_VALUE)
    col = lax.broadcasted_iota(jnp.int32, (1, nk), 1)
    is_first = (i == 0).astype(F32)
    is_last = ((i + 1) * tq >= sub).astype(F32)
    before_start = jnp.where(col < rad, -MASK_VALUE, 0.0) * is_first
    after_end = jnp.where(col >= 2 * rad, -MASK_VALUE, 0.0) * is_last
    lane = lax.broadcasted_iota(jnp.int32, (rad, LANES), 1)
    low = lane < HEAD_DIM_A
    lowf = low.astype(F32)
    n_pairs = HEADS_PER_GROUP // 2

    def scores(p):
        sl = slice(LANES * p, LANES * (p + 1))
        kc = jnp.concatenate([kp_ref[:, sl], km_ref[:, sl], kn_ref[:, sl]], axis=0)
        pen = jnp.concatenate([slopes[2 * p + hh] * absrel + out_of_band for hh in range(2)], axis=0)
        out = []
        for sb in range(n_sb):
            q2 = q_ref[sb * rad:(sb + 1) * rad, sl].astype(F32)
            qm = jnp.concatenate([q2 * lowf, q2 * (1.0 - lowf)], axis=0).astype(BF16)
            sc = lax.dot_general(qm, kc[sb * rad:sb * rad + nk], NT_DIMS, preferred_element_type=F32) - pen
            if sb == 0:
                sc = sc - before_start
            if sb == n_sb - 1:
                sc = sc - after_end
            out.append(sc)
        return out

    def finish(p, scs):
        sl = slice(LANES * p, LANES * (p + 1))
        vc = jnp.concatenate([vp_ref[:, sl], vm_ref[:, sl], vn_ref[:, sl]], axis=0)
        for sb, sc in enumerate(scs):
            rows = slice(sb * rad, (sb + 1) * rad)
            m = jnp.max(sc, axis=-1, keepdims=True)
            e = jnp.exp(sc - m)
            l = jnp.sum(e, axis=-1, keepdims=True)
            o = _dot(e.astype(BF16), vc[sb * rad:sb * rad + nk]) / l
            lse = m + jnp.log(l)
            o_ref[rows, sl] = jnp.where(low, o[:rad], o[rad:]).astype(BF16)
            rest = 0.0 if p == 0 else lse_ref[rows, :]
            lse_ref[rows, :] = jnp.where(lane == 2 * p, lse[:rad], jnp.where(lane == 2 * p + 1, lse[rad:], rest))

    pending = scores(0)
    for p in range(1, n_pairs):
        upcoming = scores(p)
        finish(p - 1, pending)
        pending = upcoming
    finish(n_pairs - 1, pending)


def _band_attention(qkv, group, dil, tq=512):
    b, _, sub, _ = qkv.shape
    w = ATTN_WIDTH
    tq = min(tq, sub)
    hb = tq // BAND_RADIUS
    n_halo = sub // BAND_RADIUS
    n_heads = len(DIL_PATTERNS) * HEADS_PER_GROUP
    slopes = tuple(float(2.0 ** (-8.0 * (group * HEADS_PER_GROUP + hd + 1) / n_heads)) * dil
                   for hd in range(HEADS_PER_GROUP))

    def main(col):
        return pl.BlockSpec((None, None, tq, w), lambda bi, r, i: (bi, r, i, col))

    def prev(col):
        return pl.BlockSpec((None, None, BAND_RADIUS, w),
                            lambda bi, r, i: (bi, r, jnp.maximum(i * hb - 1, 0), col))

    def nxt(col):
        return pl.BlockSpec((None, None, BAND_RADIUS, w),
                            lambda bi, r, i: (bi, r, jnp.minimum((i + 1) * hb, n_halo - 1), col))

    return pl.pallas_call(
        functools.partial(_band_body, tq=tq, sub=sub, slopes=slopes),
        grid=(b, dil, sub // tq),
        in_specs=[main(0), prev(1), main(1), nxt(1), prev(2), main(2), nxt(2)],
        out_specs=[main(0), pl.BlockSpec((None, None, tq, LANES), lambda bi, r, i: (bi, r, i, 0))],
        out_shape=[jax.ShapeDtypeStruct((b, dil, sub, w), BF16), jax.ShapeDtypeStruct((b, dil, sub, LANES), F32)],
        compiler_params=_params("parallel", "parallel", "parallel"),
        name=f"band_attn_d{dil}",
    )(qkv, qkv, qkv, qkv, qkv, qkv, qkv)


def _attn_out_block(i, rs, ins, scratch, x_ref):
    o1, o4, o16, l1, l4, l16, h_ref, w_ref = ins
    os_ref, ls_ref = scratch
    blk = rs.stop - rs.start
    n_lane_groups = ATTN_WIDTH // LANES
    for grp, (o_ref, l_ref, (_, dil)) in enumerate(zip((o1, o4, o16), (l1, l4, l16), DIL_PATTERNS)):
        src = slice(i * (blk // dil), (i + 1) * (blk // dil))
        for r in range(dil):
            rows = pl.ds(r, blk // dil, stride=dil) if dil > 1 else slice(None)
            ls_ref[grp, rows, :] = l_ref[r, src, :]
            for c in range(n_lane_groups):
                os_ref[grp, c, rows, :] = o_ref[r, src, c * LANES:(c + 1) * LANES].astype(F32)
        yield
    ls = [ls_ref[grp] for grp in range(len(DIL_PATTERNS))]
    m = jnp.maximum(jnp.maximum(ls[0], ls[1]), ls[2])
    es = [jnp.exp(l - m) for l in ls]
    inv = 1.0 / (es[0] + es[1] + es[2])
    row = lax.broadcasted_iota(jnp.int32, (LANES, ATTN_WIDTH), 0)
    col = lax.broadcasted_iota(jnp.int32, (LANES, ATTN_WIDTH), 1)
    spread = jnp.where(col // HEAD_DIM_A == row, 1.0, 0.0).astype(BF16)
    yield
    o = 0.0
    for grp, e in enumerate(es):
        wgt = e * inv
        hi = wgt.astype(BF16)
        lo = (wgt - hi.astype(F32)).astype(BF16)
        full = _dot(hi, spread) + _dot(lo, spread)
        o = o + full * jnp.concatenate([os_ref[grp, c] for c in range(n_lane_groups)], axis=1)
        yield
    x_ref[rs, :] = h_ref[rs, :] + _dot(o.astype(BF16), w_ref[...])


def _attn_out_ffn(os_, lses, h, w_o, ffn_params, *, final, tm=512, n_sub=2):
    b, s, d = h.shape
    w = ATTN_WIDTH
    specs = [pl.BlockSpec((None, dil, tm // dil, w), lambda bi, i: (bi, 0, i, 0)) for _, dil in DIL_PATTERNS]
    lspecs = [pl.BlockSpec((None, dil, tm // dil, LANES), lambda bi, i: (bi, 0, i, 0)) for _, dil in DIL_PATTERNS]
    row = pl.BlockSpec((None, tm, d), lambda bi, i: (bi, i, 0))
    n_grp = len(DIL_PATTERNS)
    blk = tm // n_sub
    return _mixer_ffn(
        _attn_out_block, (*os_, *lses, h, w_o), specs + lspecs + [row, _resident((w, d))],
        [pltpu.VMEM((n_grp, w // LANES, blk, LANES), F32), pltpu.VMEM((n_grp, blk, LANES), F32)],
        ffn_params, grid=(b, s // tm), out_spec=row, out_shape=jax.ShapeDtypeStruct((b, s, d), F32),
        tm=tm, final=final, name="attn_out_ffn", n_sub=n_sub)


def _attention_mixer_ffn(h, g, w_qkv, w_o, ffn_params, *, final):
    w = ATTN_WIDTH
    col = jnp.arange(w_qkv.shape[1]) % (3 * w)
    wq = (w_qkv * jnp.where(col < w, HEAD_DIM_A ** -0.5, 1.0)).astype(BF16)
    qkvs = _attn_proj(h, g, wq)
    os_, lses = [], []
    for grp, (_, dil) in enumerate(DIL_PATTERNS):
        o, lse = _band_attention(qkvs[grp], grp, dil)
        os_.append(o)
        lses.append(lse)
    return _attn_out_ffn(os_, lses, h, w_o.astype(BF16), ffn_params, final=final)


def _conv_body(xp_ref, xm_ref, xn_ref, g_ref, w1_ref, b1_ref, wdw_ref, bdw_ref, lng_ref, lnb_ref, w2_ref, b2_ref,
               o_ref, xext_ref, zext_ref, zs_ref, cv_ref, *, tm, n_tiles, row_chunk, lane_chunk):
    i = pl.program_id(1)
    d = xm_ref.shape[1]
    halo = CONV_HALO
    ext = tm + 2 * halo
    xext_ref[0:halo, :] = xp_ref[...]
    xext_ref[halo:halo + tm, :] = xm_ref[...]
    xext_ref[halo + tm:ext, :] = xn_ref[...]
    u = _rms(xext_ref[...], g_ref[...]).astype(BF16)
    z = _dot(u, w1_ref[...]) + b1_ref[...]
    zext_ref[...] = z[:, :d] * jax.nn.sigmoid(z[:, d:])
    zext_ref[0:halo, :] = jnp.where(i > 0, zext_ref[0:halo, :], 0.0)
    zext_ref[halo + tm:ext, :] = jnp.where(i < n_tiles - 1, zext_ref[halo + tm:ext, :], 0.0)
    span = ext - SUBLANES
    for r in range(1, SUBLANES):
        for c0 in range(0, d, LANES):
            zs_ref[r - 1, :, c0:c0 + LANES] = zext_ref[r:r + span, c0:c0 + LANES]
    first = halo - CONV_WIDTH // 2
    for c0 in range(0, d, lane_chunk):
        cs = slice(c0, c0 + lane_chunk)
        for r0 in range(0, tm, row_chunk):
            acc = jnp.zeros((row_chunk, lane_chunk), F32)
            for k in range(CONV_WIDTH):
                q, r = divmod(first + k, SUBLANES)
                rows = slice(r0 + q * SUBLANES, r0 + q * SUBLANES + row_chunk)
                tap = zext_ref[rows, cs] if r == 0 else zs_ref[r - 1, rows, cs]
                acc = acc + wdw_ref[k:k + 1, cs] * tap
            cv_ref[r0:r0 + row_chunk, cs] = acc + bdw_ref[:, cs]
    z = cv_ref[...]
    mu = jnp.mean(z, axis=-1, keepdims=True)
    zc = z - mu
    var = jnp.mean(zc * zc, axis=-1, keepdims=True)
    zn = zc * lax.rsqrt(var + EPS) * lng_ref[...] + lnb_ref[...]
    y = _silu(zn).astype(BF16)
    o_ref[...] = xm_ref[...] + _dot(y, w2_ref[...]) + b2_ref[...]


def _conv_mixer(h, g, w_pw1, b_pw1, w_dw, b_dw, ln_g, ln_b, w_pw2, b_pw2, tm=512):
    b, s, d = h.shape
    n_tiles = s // tm
    hb = tm // CONV_HALO
    n_halo = s // CONV_HALO
    ext = tm + 2 * CONV_HALO
    main = pl.BlockSpec((None, tm, d), lambda bi, i: (bi, i, 0))
    prev = pl.BlockSpec((None, CONV_HALO, d), lambda bi, i: (bi, jnp.maximum(i * hb - 1, 0), 0))
    nxt = pl.BlockSpec((None, CONV_HALO, d), lambda bi, i: (bi, jnp.minimum((i + 1) * hb, n_halo - 1), 0))
    vec = _resident((1, d))
    return pl.pallas_call(
        functools.partial(_conv_body, tm=tm, n_tiles=n_tiles, row_chunk=128, lane_chunk=256),
        grid=(b, n_tiles),
        in_specs=[prev, main, nxt, vec, _resident((d, 2 * d)), _resident((1, 2 * d)), _resident((CONV_WIDTH, d)),
                  vec, vec, vec, _resident((d, d)), vec],
        out_specs=main,
        out_shape=jax.ShapeDtypeStruct((b, s, d), F32),
        scratch_shapes=[pltpu.VMEM((ext, d), F32), pltpu.VMEM((ext, d), F32),
                        pltpu.VMEM((SUBLANES - 1, ext - SUBLANES, d), F32), pltpu.VMEM((tm, d), F32)],
        compiler_params=_params("parallel", "parallel"),
        name="conv_module",
    )(h, h, h, g, w_pw1.astype(BF16), b_pw1.reshape(1, 2 * d), w_dw, b_dw.reshape(1, d), ln_g.reshape(1, d),
      ln_b.reshape(1, d), w_pw2.astype(BF16), b_pw2.reshape(1, d))


def _pack_complex(re, im):
    hi = lax.bitcast_convert_type(re.astype(BF16).astype(F32), jnp.uint32)
    lo = lax.bitcast_convert_type(im.astype(BF16).astype(F32), jnp.uint32)
    return hi | (lo >> 16)


def _unpack_complex(word):
    re = lax.bitcast_convert_type(word & jnp.uint32(0xFFFF0000), F32)
    im = lax.bitcast_convert_type(word << 16, F32)
    return re.astype(BF16), im.astype(BF16)


def _fft1_body(x_ref, g_ref, w1_ref, ct_ref, st_ref, y_ref):
    n1, n2s, d = x_ref.shape
    rows = n1 * n2s
    u = _rms(x_ref[...].reshape(rows, d), g_ref[...]).astype(BF16)
    y = _dot(w1_ref[...], u)
    yr, ys = y[:rows], y[rows:]
    ct = jnp.concatenate([ct_ref[...].reshape(rows, LANES)] * (d // LANES), axis=1)
    st = jnp.concatenate([st_ref[...].reshape(rows, LANES)] * (d // LANES), axis=1)
    y_ref[...] = _pack_complex(yr * ct - ys * st, -(yr * st + ys * ct)).reshape(n1, n2s, d)


def _fft2_body(y_ref, h_ref, m2_ref, cc_ref, sc_ref, wf_ref, bf_ref, o_ref, zr_ref, zi_ref, mix_ref, res_ref, *,
               out_rows):
    k1s, n2, d = y_ref.shape
    for jj in range(k1s):
        re, im = _unpack_complex(y_ref[jj])
        z = _dot(m2_ref[...], jnp.concatenate([re, im], axis=0))
        zr_ref[jj * n2:(jj + 1) * n2, :] = z[:n2].astype(BF16)
        zi_ref[jj * n2:(jj + 1) * n2, :] = z[n2:].astype(BF16)
    gd = FOURIER_GROUP_DIM
    for c0 in range(0, d, gd):
        cs = slice(c0, c0 + gd)
        mix_ref[:, cs] = (_dot(zr_ref[:, cs], cc_ref[...]) + _dot(zi_ref[:, cs], sc_ref[...])).astype(BF16)
    for r0 in range(0, k1s * n2, out_rows):
        res_ref[r0:r0 + out_rows, :] = _dot(mix_ref[r0:r0 + out_rows, :], wf_ref[...]) + bf_ref[...]
    o_ref[...] = h_ref[...] + jnp.swapaxes(res_ref[...].reshape(k1s, n2, d), 0, 1)


def _dft_tables(s, n2_per_step):
    n2 = 128
    n1 = s // n2
    def cs(n):
        ang = 2.0 * np.pi * (np.outer(np.arange(n), np.arange(n)) % n) / n
        return np.cos(ang), np.sin(ang)
    c1, s1 = cs(n1)
    c2, s2 = cs(n2)
    cc, sc = cs(FOURIER_GROUP_DIM)
    eye = np.eye(n2_per_step)
    scale = 1.0 / math.sqrt(s * FOURIER_GROUP_DIM)
    ang_t = 2.0 * np.pi * (np.outer(np.arange(n1), np.arange(n2)) % s) / s
    return dict(
        n1=n1, n2=n2,
        w1=jnp.asarray(np.concatenate([np.kron(c1, eye), np.kron(s1, eye)], axis=0), F32).astype(BF16),
        m2=jnp.asarray(np.block([[c2, s2], [-s2, c2]]), F32).astype(BF16),
        cc=jnp.asarray(cc * scale, F32).astype(BF16),
        sc=jnp.asarray(sc * scale, F32).astype(BF16),
        ct=jnp.asarray(np.cos(ang_t), F32), st=jnp.asarray(np.sin(ang_t), F32))


def _fourier_mixer(h, g, w_f, b_f, n2_per_step=SUBLANES, k1_per_step=SUBLANES, out_rows=256):
    b, s, d = h.shape
    tb = _dft_tables(s, n2_per_step)
    n1, n2 = tb["n1"], tb["n2"]
    ct = jnp.broadcast_to(tb["ct"][:, :, None], (n1, n2, LANES))
    st = jnp.broadcast_to(tb["st"][:, :, None], (n1, n2, LANES))
    rows1 = n1 * n2_per_step
    xblk = pl.BlockSpec((None, n1, n2_per_step, d), lambda bi, j: (bi, 0, j, 0))
    tw = pl.BlockSpec((n1, n2_per_step, LANES), lambda bi, j: (0, j, 0))
    y = pl.pallas_call(
        _fft1_body,
        grid=(b, n2 // n2_per_step),
        in_specs=[xblk, _resident((1, d)), _resident((2 * rows1, rows1)), tw, tw],
        out_specs=xblk,
        out_shape=jax.ShapeDtypeStruct((b, n1, n2, d), jnp.uint32),
        compiler_params=_params("parallel", "parallel"),
        name="fft_stage1",
    )(h.reshape(b, n1, n2, d), g, tb["w1"], ct, st)
    gd = FOURIER_GROUP_DIM
    rows2 = k1_per_step * n2
    oblk = pl.BlockSpec((None, n2, k1_per_step, d), lambda bi, k: (bi, 0, k, 0))
    out = pl.pallas_call(
        functools.partial(_fft2_body, out_rows=out_rows),
        grid=(b, n1 // k1_per_step),
        in_specs=[pl.BlockSpec((None, k1_per_step, n2, d), lambda bi, k: (bi, k, 0, 0)), oblk,
                  _resident((2 * n2, 2 * n2)), _resident((gd, gd)), _resident((gd, gd)),
                  _resident((d, d)), _resident((1, d))],
        out_specs=oblk,
        out_shape=jax.ShapeDtypeStruct((b, n2, n1, d), F32),
        scratch_shapes=[pltpu.VMEM((rows2, d), BF16)] * 3 + [pltpu.VMEM((rows2, d), F32)],
        compiler_params=_params("parallel", "parallel"),
        name="fft_stage2",
    )(y, h.reshape(b, n2, n1, d), tb["m2"], tb["cc"], tb["sc"], w_f.astype(BF16), b_f.reshape(1, d))
    return out.reshape(b, s, d)


def _gla_proj_body(x_ref, g_ref, win_ref, wa1_ref, wa2_ref, ba_ref, q_ref, k_ref, v_ref, r_ref, la_ref, *, n_sub):
    tm = x_ref.shape[0]
    dk = q_ref.shape[1]
    dv = v_ref.shape[1]
    rows = [slice(i * (tm // n_sub), (i + 1) * (tm // n_sub)) for i in range(n_sub)]
    us = [_rms(x_ref[rs, :], g_ref[...]).astype(BF16) for rs in rows]
    lows = [_dot(u, wa1_ref[...]).astype(BF16) for u in us]
    for rs, low in zip(rows, lows):
        z = _dot(low, wa2_ref[...]) + ba_ref[...]
        log_sig = jnp.minimum(z, 0.0) - jnp.log(1.0 + jnp.exp(-jnp.abs(z)))
        la_ref[rs, :] = log_sig / GLA_TAU
    for rs, u in zip(rows, us):
        proj = _dot(u, win_ref[...])
        q_ref[rs, :] = (proj[:, :dk] * (GLA_HEAD_K ** -0.5)).astype(BF16)
        k_ref[rs, :] = proj[:, dk:2 * dk].astype(BF16)
        v_ref[rs, :] = proj[:, 2 * dk:2 * dk + dv].astype(BF16)
        r_ref[rs, :] = proj[:, 2 * dk + dv:].astype(BF16)


def _gla_proj(h2, g, w_in, wa1, wa2, ba, tm=512, n_sub=2):
    t, d = h2.shape
    row = lambda n: pl.BlockSpec((tm, n), lambda i: (i, 0))
    rank2 = wa1.shape[1]
    return pl.pallas_call(
        functools.partial(_gla_proj_body, n_sub=n_sub),
        grid=(t // tm,),
        in_specs=[row(d), _resident((1, d)), _resident(w_in.shape), _resident((d, rank2)),
                  _resident((rank2, 2 * GLA_DK)), _resident((1, 2 * GLA_DK))],
        out_specs=[row(GLA_DK), row(GLA_DK), row(GLA_DV), row(GLA_DV), row(2 * GLA_DK)],
        out_shape=[jax.ShapeDtypeStruct((t, GLA_DK), BF16), jax.ShapeDtypeStruct((t, GLA_DK), BF16),
                   jax.ShapeDtypeStruct((t, GLA_DV), BF16), jax.ShapeDtypeStruct((t, GLA_DV), BF16),
                   jax.ShapeDtypeStruct((t, 2 * GLA_DK), F32)],
        compiler_params=_params("parallel"),
        name="gla_proj",
    )(h2, g, w_in, wa1, wa2, ba)


def _gla_prepare(q_ref, k_ref, la_ref, *, backward, n_chunks):
    c = GLA_CHUNK
    rows = n_chunks * c
    ri = lax.broadcasted_iota(jnp.int32, (rows, rows), 0)
    ci = lax.broadcasted_iota(jnp.int32, (rows, rows), 1)
    same_chunk = (ri // c) == (ci // c)
    if backward:
        tri = same_chunk & (ci >= ri)
        keep = same_chunk & (ci > ri)
    else:
        tri = same_chunk & (ci <= ri)
        keep = tri
    tri = jnp.where(tri, 1.0, 0.0).astype(BF16)
    la = la_ref[...]
    la_hi = la.astype(BF16)
    la_lo = (la - la_hi.astype(F32)).astype(BF16)
    cum = _dot(tri, la_hi) + _dot(tri, la_lo)
    ref_row = c // 2 if backward else c // 2 - 1
    last_row = 0 if backward else c - 1
    lasts = [cum[a * c + last_row:a * c + last_row + 1] for a in range(n_chunks)]
    ref = jnp.concatenate([jnp.broadcast_to(cum[a * c + ref_row:a * c + ref_row + 1], (c, cum.shape[1]))
                           for a in range(n_chunks)], axis=0)
    last = jnp.concatenate([jnp.broadcast_to(l, (c, cum.shape[1])) for l in lasts], axis=0)
    q = q_ref[...].astype(F32)
    k = k_ref[...].astype(F32)
    return dict(
        keep=keep, lasts=lasts, order=range(n_chunks - 1, -1, -1) if backward else range(n_chunks),
        q_in=(q * jnp.exp(cum - ref)).astype(BF16), k_in=(k * jnp.exp(ref - cum)).astype(BF16),
        q_x=(q * jnp.exp(cum)).astype(BF16), k_x=(k * jnp.exp(last - cum)).astype(BF16))


def _gla_scan_body(qf, kf, vf, laf, qb, kb, vb, lab, of_ref, ob_ref, stf_ref, stb_ref, accf_ref, accb_ref, *,
                   chunks_per_step):
    @pl.when(pl.program_id(1) == 0)
    def _():
        stf_ref[...] = jnp.zeros_like(stf_ref)
        stb_ref[...] = jnp.zeros_like(stb_ref)
    c = GLA_CHUNK
    heads = [(slice(GLA_HEAD_K * hd, GLA_HEAD_K * (hd + 1)), slice(GLA_HEAD_V * hd, GLA_HEAD_V * (hd + 1)))
             for hd in range(GLA_HEADS)]
    dirs = [(_gla_prepare(qf, kf, laf, backward=False, n_chunks=chunks_per_step), vf, of_ref, stf_ref, accf_ref),
            (_gla_prepare(qb, kb, lab, backward=True, n_chunks=chunks_per_step), vb, ob_ref, stb_ref, accb_ref)]
    scores = [[lax.dot_general(p["q_in"][:, ks], p["k_in"][:, ks], NT_DIMS, preferred_element_type=F32)
               for ks, _ in heads] for p, *_ in dirs]
    for (p, v_ref, _, _, acc_ref), scs in zip(dirs, scores):
        for (_, vs), sc in zip(heads, scs):
            acc_ref[:, vs] = _dot(jnp.where(p["keep"], sc, 0.0).astype(BF16), v_ref[:, vs])
    states = [[st_ref[hd] for hd in range(GLA_HEADS)] for _, _, _, st_ref, _ in dirs]
    for step in range(chunks_per_step):
        for di, (p, v_ref, o_ref, _, acc_ref) in enumerate(dirs):
            a = p["order"][step]
            rs = slice(a * c, (a + 1) * c)
            for hd, (ks, vs) in enumerate(heads):
                state_t = states[di][hd]
                inter = lax.dot_general(p["q_x"][rs, ks], state_t.astype(BF16), NT_DIMS, preferred_element_type=F32)
                o_ref[rs, vs] = (acc_ref[rs, vs] + inter).astype(o_ref.dtype)
                states[di][hd] = state_t * jnp.exp(p["lasts"][a][:, ks]) + lax.dot_general(
                    v_ref[rs, vs], p["k_x"][rs, ks], TN_DIMS, preferred_element_type=F32)
    for di, (_, _, _, st_ref, _) in enumerate(dirs):
        for hd in range(GLA_HEADS):
            st_ref[hd] = states[di][hd]


def _gla_scan(q, k, v, la, b, s, chunks_per_step=4):
    rows = chunks_per_step * GLA_CHUNK
    nb = s // rows
    fwd = lambda n, col=0: pl.BlockSpec((None, rows, n), lambda bi, i: (bi, i, col))
    bwd = lambda n, col=0: pl.BlockSpec((None, rows, n), lambda bi, i: (bi, nb - 1 - i, col))
    q3, k3, v3 = (a.reshape(b, s, a.shape[-1]) for a in (q, k, v))
    la3 = la.reshape(b, s, 2 * GLA_DK)
    state = pltpu.VMEM((GLA_HEADS, GLA_HEAD_V, GLA_HEAD_K), F32)
    return pl.pallas_call(
        functools.partial(_gla_scan_body, chunks_per_step=chunks_per_step),
        grid=(b, nb),
        in_specs=[fwd(GLA_DK), fwd(GLA_DK), fwd(GLA_DV), fwd(GLA_DK, 0),
                  bwd(GLA_DK), bwd(GLA_DK), bwd(GLA_DV), bwd(GLA_DK, 1)],
        out_specs=[fwd(GLA_DV), bwd(GLA_DV)],
        out_shape=[jax.ShapeDtypeStruct((b, s, GLA_DV), BF16)] * 2,
        scratch_shapes=[state, state, pltpu.VMEM((rows, GLA_DV), F32), pltpu.VMEM((rows, GLA_DV), F32)],
        compiler_params=_params("parallel", "arbitrary"),
        name="gla_scan",
    )(q3, k3, v3, la3, q3, k3, v3, la3)


def _gla_out_block(i, rs, ins, scratch, x_ref):
    of_ref, ob_ref, r_ref, ng_ref, h_ref, w_ref = ins
    parts = []
    for hd in range(GLA_HEADS):
        vs = slice(GLA_HEAD_V * hd, GLA_HEAD_V * (hd + 1))
        oh = of_ref[rs, vs].astype(F32) + ob_ref[rs, vs].astype(F32)
        oh = oh * lax.rsqrt(jnp.mean(oh * oh, axis=-1, keepdims=True) + EPS)
        parts.append((oh * ng_ref[:, vs] * _silu(r_ref[rs, vs].astype(F32))).astype(BF16))
        yield
    x_ref[rs, :] = h_ref[rs, :] + _dot(jnp.concatenate(parts, axis=1), w_ref[...])


def _gla_out_ffn(o_f, o_b, r, norm_g, h2, w_o, ffn_params, *, final, tm=512, n_sub=2):
    t, d = h2.shape
    row = lambda n: pl.BlockSpec((tm, n), lambda i: (i, 0))
    return _mixer_ffn(
        _gla_out_block, (o_f.reshape(t, GLA_DV), o_b.reshape(t, GLA_DV), r, norm_g, h2, w_o),
        [row(GLA_DV), row(GLA_DV), row(GLA_DV), _resident((1, GLA_DV)), row(d), _resident((GLA_DV, d))], [],
        ffn_params, grid=(t // tm,), out_spec=row(d), out_shape=jax.ShapeDtypeStruct((t, d), F32),
        tm=tm, final=final, name="gla_out_ffn", n_sub=n_sub)


def _gla_mixer_ffn(h, g, w_in, w_a1, w_a2, b_a, norm_g, w_o, ffn_params, *, final):
    b, s, d = h.shape
    rank = w_a1.shape[-1]
    wa1 = jnp.concatenate([w_a1[0], w_a1[1]], axis=1).astype(BF16)
    zeros = jnp.zeros((rank, GLA_DK), F32)
    wa2 = jnp.concatenate([jnp.concatenate([w_a2[0], zeros], axis=1),
                           jnp.concatenate([zeros, w_a2[1]], axis=1)], axis=0).astype(BF16)
    ba = jnp.concatenate([b_a[0], b_a[1]]).reshape(1, 2 * GLA_DK)
    h2 = h.reshape(b * s, d)
    q, k, v, r, la = _gla_proj(h2, g, w_in.astype(BF16), wa1, wa2, ba)
    o_f, o_b = _gla_scan(q, k, v, la, b, s)
    return _gla_out_ffn(o_f, o_b, r, norm_g.reshape(1, GLA_DV), h2, w_o.astype(BF16), ffn_params,
                        final=final).reshape(b, s, d)


def kernel(x, norm_g, final_norm_g, ffn_w1, ffn_w3, ffn_w2, attn_w_qkv, attn_w_o, conv_w_pw1, conv_b_pw1,
           conv_w_dw, conv_b_dw, conv_ln_g, conv_ln_b, conv_w_pw2, conv_b_pw2, fnet_w, fnet_b, gla_w_in,
           gla_w_a1, gla_w_a2, gla_b_a, gla_norm_g, gla_w_o):
    b, s, d = x.shape
    depth = norm_g.shape[0]
    n_mixers = 4
    gf = final_norm_g.reshape(1, d)
    h = x

    def ffn_params(i, half):
        return (norm_g[i, 2 * half].reshape(1, d), ffn_w1[i, half].astype(BF16), ffn_w3[i, half].astype(BF16),
                ffn_w2[i, half].astype(BF16), gf)

    def ffn(h, i, half, final=False):
        return _ffn(h.reshape(b * s, d), *ffn_params(i, half), final=final).reshape(b, s, d)

    for i in range(depth):
        m, j = i % n_mixers, i // n_mixers
        final = i == depth - 1
        h = ffn(h, i, 0)
        g = norm_g[i, 1].reshape(1, d)
        post = ffn_params(i, 1)
        if m == 0:
            h = _attention_mixer_ffn(h, g, attn_w_qkv[j], attn_w_o[j], post, final=final)
        elif m == 1:
            h = _conv_mixer(h, g, conv_w_pw1[j], conv_b_pw1[j], conv_w_dw[j], conv_b_dw[j], conv_ln_g[j],
                            conv_ln_b[j], conv_w_pw2[j], conv_b_pw2[j])
            h = ffn(h, i, 1, final=final)
        elif m == 2:
            h = _fourier_mixer(h, g, fnet_w[j], fnet_b[j])
            h = ffn(h, i, 1, final=final)
        else:
            h = _gla_mixer_ffn(h, g, gla_w_in[j], gla_w_a1[j], gla_w_a2[j], gla_b_a[j], gla_norm_g[j],
                               gla_w_o[j], post, final=final)
    return h
```

```python
import functools
import math

import numpy as np
import jax
import jax.numpy as jnp
from jax import lax
from jax.experimental import pallas as pl
from jax.experimental.pallas import tpu as pltpu

F32 = jnp.float32
BF16 = jnp.bfloat16
EPS = 1e-6
MASK_VALUE = -1e30

LANES = 128
SUBLANES = 8
VMEM_LIMIT_BYTES = 56 * 1024 * 1024

DIL_PATTERNS = ((128, 1), (512, 4), (2048, 16))
HEADS_PER_GROUP = 8
HEAD_DIM_A = 64
ATTN_WIDTH = HEADS_PER_GROUP * HEAD_DIM_A
BAND_RADIUS = 64
CONV_WIDTH = 31
CONV_HALO = 16
FOURIER_GROUP_DIM = 256
GLA_HEADS = 4
GLA_HEAD_K = 128
GLA_HEAD_V = 256
GLA_DK = GLA_HEADS * GLA_HEAD_K
GLA_DV = GLA_HEADS * GLA_HEAD_V
GLA_TAU = 16.0
GLA_CHUNK = 64

NT_DIMS = (((1,), (1,)), ((), ()))
TN_DIMS = (((0,), (0,)), ((), ()))


def _params(*semantics):
    return pltpu.CompilerParams(dimension_semantics=semantics, vmem_limit_bytes=VMEM_LIMIT_BYTES)


def _resident(shape):
    nd = len(shape)
    return pl.BlockSpec(shape, lambda *_: (0,) * nd, pipeline_mode=pl.Buffered(1))


def _rms(x, g):
    ms = jnp.mean(x * x, axis=-1, keepdims=True)
    return x * lax.rsqrt(ms + EPS) * g


def _silu(x):
    return x * jax.nn.sigmoid(x)


def _dot(a, b):
    return jnp.dot(a, b, preferred_element_type=F32)


def _ffn_blocks(block_input, x_ref, g_ref, w1_ref, w3_ref, w2_ref, gf_ref, o_ref, gate_ref, *, ff_chunk, n_sub, final,
                pieces_per_chunk=1):
    tm, d = o_ref.shape
    d_ff = w1_ref.shape[1]
    rows = [slice(i * (tm // n_sub), (i + 1) * (tm // n_sub)) for i in range(n_sub)]
    stages = [block_input(i, rs) if block_input is not None else iter(()) for i, rs in enumerate(rows)]
    for _ in stages[0]:
        pass
    for i, rs in enumerate(rows):
        ahead = stages[i + 1] if i + 1 < n_sub else iter(())
        hn = _rms(x_ref[rs, :], g_ref[...]).astype(BF16)
        for c0 in range(0, d_ff, ff_chunk):
            sl = slice(c0, c0 + ff_chunk)
            a = _dot(hn, w1_ref[:, sl])
            b = _dot(hn, w3_ref[:, sl])
            gate_ref[rs, sl] = (_silu(a) * b).astype(BF16)
            for _ in range(pieces_per_chunk):
                next(ahead, None)
        for _ in ahead:
            pass
    for rs in rows:
        if final:
            y = x_ref[rs, :] + 0.5 * _dot(gate_ref[rs, :], w2_ref[...])
            o_ref[rs, :] = _rms(y, gf_ref[...])
        else:
            for c0 in range(0, d, ff_chunk):
                sl = slice(c0, c0 + ff_chunk)
                o_ref[rs, sl] = x_ref[rs, sl] + 0.5 * _dot(gate_ref[rs, :], w2_ref[:, sl])


def _ffn_body(x_ref, g_ref, w1_ref, w3_ref, w2_ref, gf_ref, o_ref, gate_ref, *, ff_chunk, n_sub, final):
    _ffn_blocks(None, x_ref, g_ref, w1_ref, w3_ref, w2_ref, gf_ref, o_ref, gate_ref,
                ff_chunk=ff_chunk, n_sub=n_sub, final=final)


def _mixer_ffn_body(*refs, n_in, block_input, ff_chunk, n_sub, final, pieces_per_chunk):
    ins = refs[:n_in]
    g_ref, w1_ref, w3_ref, w2_ref, gf_ref, o_ref, x_ref, gate_ref = refs[n_in:n_in + 8]
    scratch = refs[n_in + 8:]
    _ffn_blocks(lambda i, rs: block_input(i, rs, ins, scratch, x_ref), x_ref, g_ref, w1_ref, w3_ref, w2_ref,
                gf_ref, o_ref, gate_ref, ff_chunk=ff_chunk, n_sub=n_sub, final=final,
                pieces_per_chunk=pieces_per_chunk)


def _mixer_ffn(block_input, inputs, in_specs, scratch_shapes, ffn_params, *, grid, out_spec, out_shape, tm, final,
               name, ff_chunk=256, n_sub=2, pieces_per_chunk=1):
    g, w1, w3, w2, gf = ffn_params
    d, d_ff = w1.shape
    return pl.pallas_call(
        functools.partial(_mixer_ffn_body, n_in=len(inputs), block_input=block_input, ff_chunk=ff_chunk,
                          n_sub=n_sub, final=final, pieces_per_chunk=pieces_per_chunk),
        grid=grid,
        in_specs=list(in_specs) + [_resident((1, d)), _resident((d, d_ff)), _resident((d, d_ff)),
                                   _resident((d_ff, d)), _resident((1, d))],
        out_specs=out_spec,
        out_shape=out_shape,
        scratch_shapes=[pltpu.VMEM((tm, d), F32), pltpu.VMEM((tm, d_ff), BF16)] + list(scratch_shapes),
        compiler_params=_params(*(("parallel",) * len(grid))),
        name=name,
    )(*inputs, g, w1, w3, w2, gf)


def _ffn(h2, g, w1, w3, w2, gf, *, final, tm=1024, ff_chunk=256, n_sub=2):
    t, d = h2.shape
    d_ff = w1.shape[1]
    row = pl.BlockSpec((tm, d), lambda i: (i, 0))
    return pl.pallas_call(
        functools.partial(_ffn_body, ff_chunk=ff_chunk, n_sub=n_sub, final=final),
        grid=(t // tm,),
        in_specs=[row, _resident((1, d)), _resident((d, d_ff)), _resident((d, d_ff)),
                  _resident((d_ff, d)), _resident((1, d))],
        out_specs=row,
        out_shape=jax.ShapeDtypeStruct((t, d), F32),
        scratch_shapes=[pltpu.VMEM((tm, d_ff), BF16)],
        compiler_params=_params("parallel"),
        name="ffn_final" if final else "ffn",
    )(h2, g, w1, w3, w2, gf)


def _attn_proj_body(x_ref, g_ref, w_ref, o1_ref, o4_ref, o16_ref, y_ref, *, tm, n_sub):
    n = 3 * ATTN_WIDTH
    o_refs = (o1_ref, o4_ref, o16_ref)
    blk = tm // n_sub
    normed = {}

    def project(i, grp):
        if i not in normed:
            normed[i] = _rms(x_ref[i * blk:(i + 1) * blk, :], g_ref[...]).astype(BF16)
        return _dot(normed[i], w_ref[:, grp * n:(grp + 1) * n])

    def emit(i, grp, y):
        o_ref, dil = o_refs[grp], DIL_PATTERNS[grp][1]
        out_rows = slice(i * (blk // dil), (i + 1) * (blk // dil))
        if dil == 1:
            o_ref[0, out_rows, :] = y.astype(BF16)
            return
        for c in range(n // LANES):
            y_ref[c] = y[:, c * LANES:(c + 1) * LANES]
        for r in range(dil):
            for c in range(n // LANES):
                o_ref[r, out_rows, c * LANES:(c + 1) * LANES] = (
                    y_ref[c, pl.ds(r, blk // dil, stride=dil), :].astype(BF16))

    work = [(i, grp) for i in range(n_sub) for grp in range(len(DIL_PATTERNS))]
    y = project(*work[0])
    for j, item in enumerate(work):
        y_next = project(*work[j + 1]) if j + 1 < len(work) else None
        emit(*item, y)
        y = y_next


def _attn_proj(h, g, w, tm=512, n_sub=1):
    b, s, d = h.shape
    n = 3 * ATTN_WIDTH
    out_specs, out_shape = [], []
    for _, dil in DIL_PATTERNS:
        out_specs.append(pl.BlockSpec((None, dil, tm // dil, n), lambda bi, i: (bi, 0, i, 0)))
        out_shape.append(jax.ShapeDtypeStruct((b, dil, s // dil, n), BF16))
    return pl.pallas_call(
        functools.partial(_attn_proj_body, tm=tm, n_sub=n_sub),
        grid=(b, s // tm),
        in_specs=[pl.BlockSpec((None, tm, d), lambda bi, i: (bi, i, 0)), _resident((1, d)),
                  _resident((d, len(DIL_PATTERNS) * n))],
        out_specs=out_specs,
        out_shape=out_shape,
        scratch_shapes=[pltpu.VMEM((n // LANES, tm // n_sub, LANES), F32)],
        compiler_params=_params("parallel", "parallel"),
        name="attn_proj",
    )(h, g, w)


def _band_body(q_ref, kp_ref, km_ref, kn_ref, vp_ref, vm_ref, vn_ref, o_ref, lse_ref, *, tq, sub, slopes):
    i = pl.program_id(2)
    rad = BAND_RADIUS
    nk = 3 * rad
    n_sb = tq // rad
    r = lax.broadcasted_iota(jnp.int32, (rad, nk), 0)
    c = lax.broadcasted_iota(jnp.int32, (rad, nk), 1)
    dist = jnp.abs(c - rad - r)
    absrel = dist.astype(F32)
    out_of_band = jnp.where(dist <= rad, 0.0, -MASK_VALUE)
    col = lax.broadcasted_iota(jnp.int32, (1, nk), 1)
    is_first = (i == 0).astype(F32)
    is_last = ((i + 1) * tq >= sub).astype(F32)
    before_start = jnp.where(col < rad, -MASK_VALUE, 0.0) * is_first
    after_end = jnp.where(col >= 2 * rad, -MASK_VALUE, 0.0) * is_last
    lane = lax.broadcasted_iota(jnp.int32, (rad, LANES), 1)
    low = lane < HEAD_DIM_A
    lowf = low.astype(F32)
    n_pairs = HEADS_PER_GROUP // 2

    def scores(p):
        sl = slice(LANES * p, LANES * (p + 1))
        kc = jnp.concatenate([kp_ref[:, sl], km_ref[:, sl], kn_ref[:, sl]], axis=0)
        pen = jnp.concatenate([slopes[2 * p + hh] * absrel + out_of_band for hh in range(2)], axis=0)
        out = []
        for sb in range(n_sb):
            q2 = q_ref[sb * rad:(sb + 1) * rad, sl].astype(F32)
            qm = jnp.concatenate([q2 * lowf, q2 * (1.0 - lowf)], axis=0).astype(BF16)
            sc = lax.dot_general(qm, kc[sb * rad:sb * rad + nk], NT_DIMS, preferred_element_type=F32) - pen
            if sb == 0:
                sc = sc - before_start
            if sb == n_sb - 1:
                sc = sc - after_end
            out.append(sc)
        return out

    def finish(p, scs):
        sl = slice(LANES * p, LANES * (p + 1))
        vc = jnp.concatenate([vp_ref[:, sl], vm_ref[:, sl], vn_ref[:, sl]], axis=0)
        for sb, sc in enumerate(scs):
            rows = slice(sb * rad, (sb + 1) * rad)
            m = jnp.max(sc, axis=-1, keepdims=True)
            e = jnp.exp(sc - m)
            l = jnp.sum(e, axis=-1, keepdims=True)
            o = _dot(e.astype(BF16), vc[sb * rad:sb * rad + nk]) / l
            lse = m + jnp.log(l)
            o_ref[rows, sl] = jnp.where(low, o[:rad], o[rad:]).astype(BF16)
            rest = 0.0 if p == 0 else lse_ref[rows, :]
            lse_ref[rows, :] = jnp.where(lane == 2 * p, lse[:rad], jnp.where(lane == 2 * p + 1, lse[rad:], rest))

    pending = scores(0)
    for p in range(1, n_pairs):
        upcoming = scores(p)
        finish(p - 1, pending)
        pending = upcoming
    finish(n_pairs - 1, pending)


def _band_attention(qkv, group, dil, tq=512):
    b, _, sub, _ = qkv.shape
    w = ATTN_WIDTH
    tq = min(tq, sub)
    hb = tq // BAND_RADIUS
    n_halo = sub // BAND_RADIUS
    n_heads = len(DIL_PATTERNS) * HEADS_PER_GROUP
    slopes = tuple(float(2.0 ** (-8.0 * (group * HEADS_PER_GROUP + hd + 1) / n_heads)) * dil
                   for hd in range(HEADS_PER_GROUP))

    def main(col):
        return pl.BlockSpec((None, None, tq, w), lambda bi, r, i: (bi, r, i, col))

    def prev(col):
        return pl.BlockSpec((None, None, BAND_RADIUS, w),
                            lambda bi, r, i: (bi, r, jnp.maximum(i * hb - 1, 0), col))

    def nxt(col):
        return pl.BlockSpec((None, None, BAND_RADIUS, w),
                            lambda bi, r, i: (bi, r, jnp.minimum((i + 1) * hb, n_halo - 1), col))

    return pl.pallas_call(
        functools.partial(_band_body, tq=tq, sub=sub, slopes=slopes),
        grid=(b, dil, sub // tq),
        in_specs=[main(0), prev(1), main(1), nxt(1), prev(2), main(2), nxt(2)],
        out_specs=[main(0), pl.BlockSpec((None, None, tq, LANES), lambda bi, r, i: (bi, r, i, 0))],
        out_shape=[jax.ShapeDtypeStruct((b, dil, sub, w), BF16), jax.ShapeDtypeStruct((b, dil, sub, LANES), F32)],
        compiler_params=_params("parallel", "parallel", "parallel"),
        name=f"band_attn_d{dil}",
    )(qkv, qkv, qkv, qkv, qkv, qkv, qkv)


def _attn_out_block(i, rs, ins, scratch, x_ref):
    o1, o4, o16, l1, l4, l16, h_ref, w_ref = ins
    os_ref, ls_ref = scratch
    blk = rs.stop - rs.start
    n_lane_groups = ATTN_WIDTH // LANES
    for grp, (o_ref, l_ref, (_, dil)) in enumerate(zip((o1, o4, o16), (l1, l4, l16), DIL_PATTERNS)):
        src = slice(i * (blk // dil), (i + 1) * (blk // dil))
        for r in range(dil):
            rows = pl.ds(r, blk // dil, stride=dil) if dil > 1 else slice(None)
            ls_ref[grp, rows, :] = l_ref[r, src, :]
            for c in range(n_lane_groups):
                os_ref[grp, c, rows, :] = o_ref[r, src, c * LANES:(c + 1) * LANES].astype(F32)
        yield
    ls = [ls_ref[grp] for grp in range(len(DIL_PATTERNS))]
    m = jnp.maximum(jnp.maximum(ls[0], ls[1]), ls[2])
    es = [jnp.exp(l - m) for l in ls]
    inv = 1.0 / (es[0] + es[1] + es[2])
    row = lax.broadcasted_iota(jnp.int32, (LANES, ATTN_WIDTH), 0)
    col = lax.broadcasted_iota(jnp.int32, (LANES, ATTN_WIDTH), 1)
    spread = jnp.where(col // HEAD_DIM_A == row, 1.0, 0.0).astype(BF16)
    yield
    o = 0.0
    for grp, e in enumerate(es):
        wgt = e * inv
        hi = wgt.astype(BF16)
        lo = (wgt - hi.astype(F32)).astype(BF16)
        full = _dot(hi, spread) + _dot(lo, spread)
        o = o + full * jnp.concatenate([os_ref[grp, c] for c in range(n_lane_groups)], axis=1)
        yield
    x_ref[rs, :] = h_ref[rs, :] + _dot(o.astype(BF16), w_ref[...])


def _attn_out_ffn(os_, lses, h, w_o, ffn_params, *, final, tm=512, n_sub=2):
    b, s, d = h.shape
    w = ATTN_WIDTH
    specs = [pl.BlockSpec((None, dil, tm // dil, w), lambda bi, i: (bi, 0, i, 0)) for _, dil in DIL_PATTERNS]
    lspecs = [pl.BlockSpec((None, dil, tm // dil, LANES), lambda bi, i: (bi, 0, i, 0)) for _, dil in DIL_PATTERNS]
    row = pl.BlockSpec((None, tm, d), lambda bi, i: (bi, i, 0))
    n_grp = len(DIL_PATTERNS)
    blk = tm // n_sub
    return _mixer_ffn(
        _attn_out_block, (*os_, *lses, h, w_o), specs + lspecs + [row, _resident((w, d))],
        [pltpu.VMEM((n_grp, w // LANES, blk, LANES), F32), pltpu.VMEM((n_grp, blk, LANES), F32)],
        ffn_params, grid=(b, s // tm), out_spec=row, out_shape=jax.ShapeDtypeStruct((b, s, d), F32),
        tm=tm, final=final, name="attn_out_ffn", n_sub=n_sub)


def _attention_mixer_ffn(h, g, w_qkv, w_o, ffn_params, *, final):
    w = ATTN_WIDTH
    col = jnp.arange(w_qkv.shape[1]) % (3 * w)
    wq = (w_qkv * jnp.where(col < w, HEAD_DIM_A ** -0.5, 1.0)).astype(BF16)
    qkvs = _attn_proj(h, g, wq)
    os_, lses = [], []
    for grp, (_, dil) in enumerate(DIL_PATTERNS):
        o, lse = _band_attention(qkvs[grp], grp, dil)
        os_.append(o)
        lses.append(lse)
    return _attn_out_ffn(os_, lses, h, w_o.astype(BF16), ffn_params, final=final)


def _conv_block(i, rs, ins, scratch, x_ref, *, n_sub, n_tiles, row_chunk, lane_chunk):
    xp_ref, xm_ref, xn_ref, g_ref, w1_ref, b1_ref, wdw_ref, bdw_ref, lng_ref, lnb_ref, w2_ref, b2_ref = ins
    xext_ref, zext_ref, zs_ref, cv_ref = scratch
    t = pl.program_id(1)
    d = xm_ref.shape[1]
    halo = CONV_HALO
    blk = rs.stop - rs.start
    ext = blk + 2 * halo
    xext_ref[0:halo, :] = xp_ref[...] if i == 0 else xm_ref[rs.start - halo:rs.start, :]
    xext_ref[halo:halo + blk, :] = xm_ref[rs, :]
    xext_ref[halo + blk:ext, :] = xn_ref[...] if i == n_sub - 1 else xm_ref[rs.stop:rs.stop + halo, :]
    u = _rms(xext_ref[...], g_ref[...]).astype(BF16)
    for c0 in range(0, d, lane_chunk):
        cs = slice(c0, c0 + lane_chunk)
        za = _dot(u, w1_ref[:, cs]) + b1_ref[:, cs]
        zg = _dot(u, w1_ref[:, d + c0:d + c0 + lane_chunk]) + b1_ref[:, d + c0:d + c0 + lane_chunk]
        zext_ref[:, cs] = za * jax.nn.sigmoid(zg)
        yield
    if i == 0:
        zext_ref[0:halo, :] = jnp.where(t > 0, zext_ref[0:halo, :], 0.0)
    if i == n_sub - 1:
        zext_ref[halo + blk:ext, :] = jnp.where(t < n_tiles - 1, zext_ref[halo + blk:ext, :], 0.0)
    span = ext - SUBLANES
    for r in range(1, SUBLANES):
        for c0 in range(0, d, LANES):
            zs_ref[r - 1, :, c0:c0 + LANES] = zext_ref[r:r + span, c0:c0 + LANES]
        yield
    first = halo - CONV_WIDTH // 2
    for c0 in range(0, d, lane_chunk):
        cs = slice(c0, c0 + lane_chunk)
        for r0 in range(0, blk, row_chunk):
            acc = jnp.zeros((row_chunk, lane_chunk), F32)
            for k in range(CONV_WIDTH):
                q, r = divmod(first + k, SUBLANES)
                rows = slice(r0 + q * SUBLANES, r0 + q * SUBLANES + row_chunk)
                tap = zext_ref[rows, cs] if r == 0 else zs_ref[r - 1, rows, cs]
                acc = acc + wdw_ref[k:k + 1, cs] * tap
            cv_ref[r0:r0 + row_chunk, cs] = acc + bdw_ref[:, cs]
            yield
    z = cv_ref[...]
    mu = jnp.mean(z, axis=-1, keepdims=True)
    zc = z - mu
    var = jnp.mean(zc * zc, axis=-1, keepdims=True)
    zn = zc * lax.rsqrt(var + EPS) * lng_ref[...] + lnb_ref[...]
    y = _silu(zn).astype(BF16)
    yield
    x_ref[rs, :] = xm_ref[rs, :] + _dot(y, w2_ref[...]) + b2_ref[...]


def _conv_body(*refs, n_in, n_sub, n_tiles, glu_pieces):
    ins = refs[:n_in]
    o_ref = refs[n_in]
    scratch = refs[n_in + 1:]
    per_block = len(scratch) // n_sub
    blk = o_ref.shape[0] // n_sub
    stages = [_conv_block(i, slice(i * blk, (i + 1) * blk), ins, scratch[i * per_block:(i + 1) * per_block], o_ref,
                          n_sub=n_sub, n_tiles=n_tiles, row_chunk=128, lane_chunk=256) for i in range(n_sub)]
    for stage in stages:
        for _ in range(glu_pieces):
            next(stage)
    for stage in stages:
        for _ in stage:
            pass


def _conv_mixer(h, g, w_pw1, b_pw1, w_dw, b_dw, ln_g, ln_b, w_pw2, b_pw2, tm=512, n_sub=1):
    b, s, d = h.shape
    n_tiles = s // tm
    hb = tm // CONV_HALO
    n_halo = s // CONV_HALO
    blk = tm // n_sub
    ext = blk + 2 * CONV_HALO
    main = pl.BlockSpec((None, tm, d), lambda bi, i: (bi, i, 0))
    prev = pl.BlockSpec((None, CONV_HALO, d), lambda bi, i: (bi, jnp.maximum(i * hb - 1, 0), 0))
    nxt = pl.BlockSpec((None, CONV_HALO, d), lambda bi, i: (bi, jnp.minimum((i + 1) * hb, n_halo - 1), 0))
    vec = _resident((1, d))
    inputs = (h, h, h, g, w_pw1.astype(BF16), b_pw1.reshape(1, 2 * d), w_dw, b_dw.reshape(1, d),
              ln_g.reshape(1, d), ln_b.reshape(1, d), w_pw2.astype(BF16), b_pw2.reshape(1, d))
    block_scratch = [pltpu.VMEM((ext, d), F32), pltpu.VMEM((ext, d), F32),
                     pltpu.VMEM((SUBLANES - 1, ext - SUBLANES, d), F32), pltpu.VMEM((blk, d), F32)]
    return pl.pallas_call(
        functools.partial(_conv_body, n_in=len(inputs), n_sub=n_sub, n_tiles=n_tiles, glu_pieces=d // 256),
        grid=(b, n_tiles),
        in_specs=[prev, main, nxt, vec, _resident((d, 2 * d)), _resident((1, 2 * d)), _resident((CONV_WIDTH, d)),
                  vec, vec, vec, _resident((d, d)), vec],
        out_specs=main,
        out_shape=jax.ShapeDtypeStruct((b, s, d), F32),
        scratch_shapes=block_scratch * n_sub,
        compiler_params=_params("parallel", "parallel"),
        name="conv_module",
    )(*inputs)


def _pack_complex(re, im):
    hi = lax.bitcast_convert_type(re.astype(BF16).astype(F32), jnp.uint32)
    lo = lax.bitcast_convert_type(im.astype(BF16).astype(F32), jnp.uint32)
    return hi | (lo >> 16)


def _unpack_complex(word):
    re = lax.bitcast_convert_type(word & jnp.uint32(0xFFFF0000), F32)
    im = lax.bitcast_convert_type(word << 16, F32)
    return re.astype(BF16), im.astype(BF16)


def _fft1_body(x_ref, g_ref, w1_ref, ct_ref, st_ref, y_ref):
    n1, n2s, d = x_ref.shape
    rows = n1 * n2s
    u = _rms(x_ref[...].reshape(rows, d), g_ref[...]).astype(BF16)
    y = _dot(w1_ref[...], u)
    yr, ys = y[:rows], y[rows:]
    ct = jnp.concatenate([ct_ref[...].reshape(rows, LANES)] * (d // LANES), axis=1)
    st = jnp.concatenate([st_ref[...].reshape(rows, LANES)] * (d // LANES), axis=1)
    y_ref[...] = _pack_complex(yr * ct - ys * st, -(yr * st + ys * ct)).reshape(n1, n2s, d)


def _fft2_body(y_ref, h_ref, m2_ref, cc_ref, sc_ref, wf_ref, bf_ref, o_ref, zr_ref, zi_ref, mix_ref, res_ref, *,
               out_rows):
    k1s, n2, d = y_ref.shape
    for jj in range(k1s):
        re, im = _unpack_complex(y_ref[jj])
        z = _dot(m2_ref[...], jnp.concatenate([re, im], axis=0))
        zr_ref[jj * n2:(jj + 1) * n2, :] = z[:n2].astype(BF16)
        zi_ref[jj * n2:(jj + 1) * n2, :] = z[n2:].astype(BF16)
    gd = FOURIER_GROUP_DIM
    for c0 in range(0, d, gd):
        cs = slice(c0, c0 + gd)
        mix_ref[:, cs] = (_dot(zr_ref[:, cs], cc_ref[...]) + _dot(zi_ref[:, cs], sc_ref[...])).astype(BF16)
    for r0 in range(0, k1s * n2, out_rows):
        res_ref[r0:r0 + out_rows, :] = _dot(mix_ref[r0:r0 + out_rows, :], wf_ref[...]) + bf_ref[...]
    o_ref[...] = h_ref[...] + jnp.swapaxes(res_ref[...].reshape(k1s, n2, d), 0, 1)


def _dft_tables(s, n2_per_step):
    n2 = 128
    n1 = s // n2
    def cs(n):
        ang = 2.0 * np.pi * (np.outer(np.arange(n), np.arange(n)) % n) / n
        return np.cos(ang), np.sin(ang)
    c1, s1 = cs(n1)
    c2, s2 = cs(n2)
    cc, sc = cs(FOURIER_GROUP_DIM)
    eye = np.eye(n2_per_step)
    scale = 1.0 / math.sqrt(s * FOURIER_GROUP_DIM)
    ang_t = 2.0 * np.pi * (np.outer(np.arange(n1), np.arange(n2)) % s) / s
    return dict(
        n1=n1, n2=n2,
        w1=jnp.asarray(np.concatenate([np.kron(c1, eye), np.kron(s1, eye)], axis=0), F32).astype(BF16),
        m2=jnp.asarray(np.block([[c2, s2], [-s2, c2]]), F32).astype(BF16),
        cc=jnp.asarray(cc * scale, F32).astype(BF16),
        sc=jnp.asarray(sc * scale, F32).astype(BF16),
        ct=jnp.asarray(np.cos(ang_t), F32), st=jnp.asarray(np.sin(ang_t), F32))


def _fourier_mixer(h, g, w_f, b_f, n2_per_step=SUBLANES, k1_per_step=SUBLANES, out_rows=256):
    b, s, d = h.shape
    tb = _dft_tables(s, n2_per_step)
    n1, n2 = tb["n1"], tb["n2"]
    ct = jnp.broadcast_to(tb["ct"][:, :, None], (n1, n2, LANES))
    st = jnp.broadcast_to(tb["st"][:, :, None], (n1, n2, LANES))
    rows1 = n1 * n2_per_step
    xblk = pl.BlockSpec((None, n1, n2_per_step, d), lambda bi, j: (bi, 0, j, 0))
    tw = pl.BlockSpec((n1, n2_per_step, LANES), lambda bi, j: (0, j, 0))
    y = pl.pallas_call(
        _fft1_body,
        grid=(b, n2 // n2_per_step),
        in_specs=[xblk, _resident((1, d)), _resident((2 * rows1, rows1)), tw, tw],
        out_specs=xblk,
        out_shape=jax.ShapeDtypeStruct((b, n1, n2, d), jnp.uint32),
        compiler_params=_params("parallel", "parallel"),
        name="fft_stage1",
    )(h.reshape(b, n1, n2, d), g, tb["w1"], ct, st)
    gd = FOURIER_GROUP_DIM
    rows2 = k1_per_step * n2
    oblk = pl.BlockSpec((None, n2, k1_per_step, d), lambda bi, k: (bi, 0, k, 0))
    out = pl.pallas_call(
        functools.partial(_fft2_body, out_rows=out_rows),
        grid=(b, n1 // k1_per_step),
        in_specs=[pl.BlockSpec((None, k1_per_step, n2, d), lambda bi, k: (bi, k, 0, 0)), oblk,
                  _resident((2 * n2, 2 * n2)), _resident((gd, gd)), _resident((gd, gd)),
                  _resident((d, d)), _resident((1, d))],
        out_specs=oblk,
        out_shape=jax.ShapeDtypeStruct((b, n2, n1, d), F32),
        scratch_shapes=[pltpu.VMEM((rows2, d), BF16)] * 3 + [pltpu.VMEM((rows2, d), F32)],
        compiler_params=_params("parallel", "parallel"),
        name="fft_stage2",
    )(y, h.reshape(b, n2, n1, d), tb["m2"], tb["cc"], tb["sc"], w_f.astype(BF16), b_f.reshape(1, d))
    return out.reshape(b, s, d)


def _gla_proj_body(x_ref, g_ref, win_ref, wa1_ref, wa2_ref, ba_ref, q_ref, k_ref, v_ref, r_ref, la_ref, *, n_sub):
    tm = x_ref.shape[0]
    dk = q_ref.shape[1]
    dv = v_ref.shape[1]
    rows = [slice(i * (tm // n_sub), (i + 1) * (tm // n_sub)) for i in range(n_sub)]
    us = [_rms(x_ref[rs, :], g_ref[...]).astype(BF16) for rs in rows]
    lows = [_dot(u, wa1_ref[...]).astype(BF16) for u in us]
    for rs, low in zip(rows, lows):
        z = _dot(low, wa2_ref[...]) + ba_ref[...]
        log_sig = jnp.minimum(z, 0.0) - jnp.log(1.0 + jnp.exp(-jnp.abs(z)))
        la_ref[rs, :] = log_sig / GLA_TAU
    for rs, u in zip(rows, us):
        proj = _dot(u, win_ref[...])
        q_ref[rs, :] = (proj[:, :dk] * (GLA_HEAD_K ** -0.5)).astype(BF16)
        k_ref[rs, :] = proj[:, dk:2 * dk].astype(BF16)
        v_ref[rs, :] = proj[:, 2 * dk:2 * dk + dv].astype(BF16)
        r_ref[rs, :] = proj[:, 2 * dk + dv:].astype(BF16)


def _gla_proj(h2, g, w_in, wa1, wa2, ba, tm=512, n_sub=2):
    t, d = h2.shape
    row = lambda n: pl.BlockSpec((tm, n), lambda i: (i, 0))
    rank2 = wa1.shape[1]
    return pl.pallas_call(
        functools.partial(_gla_proj_body, n_sub=n_sub),
        grid=(t // tm,),
        in_specs=[row(d), _resident((1, d)), _resident(w_in.shape), _resident((d, rank2)),
                  _resident((rank2, 2 * GLA_DK)), _resident((1, 2 * GLA_DK))],
        out_specs=[row(GLA_DK), row(GLA_DK), row(GLA_DV), row(GLA_DV), row(2 * GLA_DK)],
        out_shape=[jax.ShapeDtypeStruct((t, GLA_DK), BF16), jax.ShapeDtypeStruct((t, GLA_DK), BF16),
                   jax.ShapeDtypeStruct((t, GLA_DV), BF16), jax.ShapeDtypeStruct((t, GLA_DV), BF16),
                   jax.ShapeDtypeStruct((t, 2 * GLA_DK), F32)],
        compiler_params=_params("parallel"),
        name="gla_proj",
    )(h2, g, w_in, wa1, wa2, ba)


def _gla_prepare(q_ref, k_ref, la_ref, *, backward, n_chunks):
    c = GLA_CHUNK
    rows = n_chunks * c
    ri = lax.broadcasted_iota(jnp.int32, (rows, rows), 0)
    ci = lax.broadcasted_iota(jnp.int32, (rows, rows), 1)
    same_chunk = (ri // c) == (ci // c)
    if backward:
        tri = same_chunk & (ci >= ri)
        keep = same_chunk & (ci > ri)
    else:
        tri = same_chunk & (ci <= ri)
        keep = tri
    tri = jnp.where(tri, 1.0, 0.0).astype(BF16)
    la = la_ref[...]
    la_hi = la.astype(BF16)
    la_lo = (la - la_hi.astype(F32)).astype(BF16)
    cum = _dot(tri, la_hi) + _dot(tri, la_lo)
    ref_row = c // 2 if backward else c // 2 - 1
    last_row = 0 if backward else c - 1
    lasts = [cum[a * c + last_row:a * c + last_row + 1] for a in range(n_chunks)]
    ref = jnp.concatenate([jnp.broadcast_to(cum[a * c + ref_row:a * c + ref_row + 1], (c, cum.shape[1]))
                           for a in range(n_chunks)], axis=0)
    last = jnp.concatenate([jnp.broadcast_to(l, (c, cum.shape[1])) for l in lasts], axis=0)
    q = q_ref[...].astype(F32)
    k = k_ref[...].astype(F32)
    return dict(
        keep=keep, lasts=lasts, order=range(n_chunks - 1, -1, -1) if backward else range(n_chunks),
        q_in=(q * jnp.exp(cum - ref)).astype(BF16), k_in=(k * jnp.exp(ref - cum)).astype(BF16),
        q_x=(q * jnp.exp(cum)).astype(BF16), k_x=(k * jnp.exp(last - cum)).astype(BF16))


def _gla_scan_body(qf, kf, vf, laf, qb, kb, vb, lab, of_ref, ob_ref, stf_ref, stb_ref, accf_ref, accb_ref, *,
                   chunks_per_step):
    @pl.when(pl.program_id(1) == 0)
    def _():
        stf_ref[...] = jnp.zeros_like(stf_ref)
        stb_ref[...] = jnp.zeros_like(stb_ref)
    c = GLA_CHUNK
    heads = [(slice(GLA_HEAD_K * hd, GLA_HEAD_K * (hd + 1)), slice(GLA_HEAD_V * hd, GLA_HEAD_V * (hd + 1)))
             for hd in range(GLA_HEADS)]
    dirs = []
    for bb in range(qf.shape[0]):
        dirs.append((_gla_prepare(qf.at[bb], kf.at[bb], laf.at[bb], backward=False, n_chunks=chunks_per_step),
                     vf.at[bb], of_ref.at[bb], stf_ref.at[bb], accf_ref.at[bb]))
        dirs.append((_gla_prepare(qb.at[bb], kb.at[bb], lab.at[bb], backward=True, n_chunks=chunks_per_step),
                     vb.at[bb], ob_ref.at[bb], stb_ref.at[bb], accb_ref.at[bb]))
    scores = [[lax.dot_general(p["q_in"][:, ks], p["k_in"][:, ks], NT_DIMS, preferred_element_type=F32)
               for ks, _ in heads] for p, *_ in dirs]
    for (p, v_ref, _, _, acc_ref), scs in zip(dirs, scores):
        for (_, vs), sc in zip(heads, scs):
            acc_ref[:, vs] = _dot(jnp.where(p["keep"], sc, 0.0).astype(BF16), v_ref[:, vs])
    states = [[st_ref[hd] for hd in range(GLA_HEADS)] for _, _, _, st_ref, _ in dirs]
    for step in range(chunks_per_step):
        for di, (p, v_ref, o_ref, _, acc_ref) in enumerate(dirs):
            a = p["order"][step]
            rs = slice(a * c, (a + 1) * c)
            for hd, (ks, vs) in enumerate(heads):
                state_t = states[di][hd]
                inter = lax.dot_general(p["q_x"][rs, ks], state_t.astype(BF16), NT_DIMS, preferred_element_type=F32)
                o_ref[rs, vs] = (acc_ref[rs, vs] + inter).astype(o_ref.dtype)
                states[di][hd] = state_t * jnp.exp(p["lasts"][a][:, ks]) + lax.dot_general(
                    v_ref[rs, vs], p["k_x"][rs, ks], TN_DIMS, preferred_element_type=F32)
    for di, (_, _, _, st_ref, _) in enumerate(dirs):
        for hd in range(GLA_HEADS):
            st_ref[hd] = states[di][hd]


def _gla_scan(q, k, v, la, b, s, chunks_per_step=4, seqs_per_step=2):
    rows = chunks_per_step * GLA_CHUNK
    nb = s // rows
    nq = seqs_per_step
    fwd = lambda n, col=0: pl.BlockSpec((nq, rows, n), lambda bi, i: (bi, i, col))
    bwd = lambda n, col=0: pl.BlockSpec((nq, rows, n), lambda bi, i: (bi, nb - 1 - i, col))
    q3, k3, v3 = (a.reshape(b, s, a.shape[-1]) for a in (q, k, v))
    la3 = la.reshape(b, s, 2 * GLA_DK)
    state = pltpu.VMEM((nq, GLA_HEADS, GLA_HEAD_V, GLA_HEAD_K), F32)
    acc = pltpu.VMEM((nq, rows, GLA_DV), F32)
    return pl.pallas_call(
        functools.partial(_gla_scan_body, chunks_per_step=chunks_per_step),
        grid=(b // nq, nb),
        in_specs=[fwd(GLA_DK), fwd(GLA_DK), fwd(GLA_DV), fwd(GLA_DK, 0),
                  bwd(GLA_DK), bwd(GLA_DK), bwd(GLA_DV), bwd(GLA_DK, 1)],
        out_specs=[fwd(GLA_DV), bwd(GLA_DV)],
        out_shape=[jax.ShapeDtypeStruct((b, s, GLA_DV), BF16)] * 2,
        scratch_shapes=[state, state, acc, acc],
        compiler_params=_params("parallel", "arbitrary"),
        name="gla_scan",
    )(q3, k3, v3, la3, q3, k3, v3, la3)


def _gla_out_block(i, rs, ins, scratch, x_ref):
    of_ref, ob_ref, r_ref, ng_ref, h_ref, w_ref = ins
    parts = []
    for hd in range(GLA_HEADS):
        vs = slice(GLA_HEAD_V * hd, GLA_HEAD_V * (hd + 1))
        oh = of_ref[rs, vs].astype(F32) + ob_ref[rs, vs].astype(F32)
        oh = oh * lax.rsqrt(jnp.mean(oh * oh, axis=-1, keepdims=True) + EPS)
        parts.append((oh * ng_ref[:, vs] * _silu(r_ref[rs, vs].astype(F32))).astype(BF16))
        yield
    x_ref[rs, :] = h_ref[rs, :] + _dot(jnp.concatenate(parts, axis=1), w_ref[...])


def _gla_out_ffn(o_f, o_b, r, norm_g, h2, w_o, ffn_params, *, final, tm=512, n_sub=2):
    t, d = h2.shape
    row = lambda n: pl.BlockSpec((tm, n), lambda i: (i, 0))
    return _mixer_ffn(
        _gla_out_block, (o_f.reshape(t, GLA_DV), o_b.reshape(t, GLA_DV), r, norm_g, h2, w_o),
        [row(GLA_DV), row(GLA_DV), row(GLA_DV), _resident((1, GLA_DV)), row(d), _resident((GLA_DV, d))], [],
        ffn_params, grid=(t // tm,), out_spec=row(d), out_shape=jax.ShapeDtypeStruct((t, d), F32),
        tm=tm, final=final, name="gla_out_ffn", n_sub=n_sub)


def _gla_mixer_ffn(h, g, w_in, w_a1, w_a2, b_a, norm_g, w_o, ffn_params, *, final):
    b, s, d = h.shape
    rank = w_a1.shape[-1]
    wa1 = jnp.concatenate([w_a1[0], w_a1[1]], axis=1).astype(BF16)
    zeros = jnp.zeros((rank, GLA_DK), F32)
    wa2 = jnp.concatenate([jnp.concatenate([w_a2[0], zeros], axis=1),
                           jnp.concatenate([zeros, w_a2[1]], axis=1)], axis=0).astype(BF16)
    ba = jnp.concatenate([b_a[0], b_a[1]]).reshape(1, 2 * GLA_DK)
    h2 = h.reshape(b * s, d)
    q, k, v, r, la = _gla_proj(h2, g, w_in.astype(BF16), wa1, wa2, ba)
    o_f, o_b = _gla_scan(q, k, v, la, b, s)
    return _gla_out_ffn(o_f, o_b, r, norm_g.reshape(1, GLA_DV), h2, w_o.astype(BF16), ffn_params,
                        final=final).reshape(b, s, d)


def kernel(x, norm_g, final_norm_g, ffn_w1, ffn_w3, ffn_w2, attn_w_qkv, attn_w_o, conv_w_pw1, conv_b_pw1,
           conv_w_dw, conv_b_dw, conv_ln_g, conv_ln_b, conv_w_pw2, conv_b_pw2, fnet_w, fnet_b, gla_w_in,
           gla_w_a1, gla_w_a2, gla_b_a, gla_norm_g, gla_w_o):
    b, s, d = x.shape
    depth = norm_g.shape[0]
    n_mixers = 4
    gf = final_norm_g.reshape(1, d)
    h = x

    def ffn_params(i, half):
        return (norm_g[i, 2 * half].reshape(1, d), ffn_w1[i, half].astype(BF16), ffn_w3[i, half].astype(BF16),
                ffn_w2[i, half].astype(BF16), gf)

    def ffn(h, i, half, final=False):
        return _ffn(h.reshape(b * s, d), *ffn_params(i, half), final=final).reshape(b, s, d)

    for i in range(depth):
        m, j = i % n_mixers, i // n_mixers
        final = i == depth - 1
        h = ffn(h, i, 0)
        g = norm_g[i, 1].reshape(1, d)
        post = ffn_params(i, 1)
        if m == 0:
            h = _attention_mixer_ffn(h, g, attn_w_qkv[j], attn_w_o[j], post, final=final)
        elif m == 1:
            h = _conv_mixer(h, g, conv_w_pw1[j], conv_b_pw1[j], conv_w_dw[j], conv_b_dw[j], conv_ln_g[j],
                            conv_ln_b[j], conv_w_pw2[j], conv_b_pw2[j])
            h = ffn(h, i, 1, final=final)
        elif m == 2:
            h = _fourier_mixer(h, g, fnet_w[j], fnet_b[j])
            h = ffn(h, i, 1, final=final)
        else:
            h = _gla_mixer_ffn(h, g, gla_w_in[j], gla_w_a1[j], gla_w_a2[j], gla_b_a[j], gla_norm_g[j],
                               gla_w_o[j], post, final=final)
    return h
```

```python
import functools
import math

import numpy as np
import jax
import jax.numpy as jnp
from jax import lax
from jax.experimental import pallas as pl
from jax.experimental.pallas import tpu as pltpu

F32 = jnp.float32
BF16 = jnp.bfloat16
EPS = 1e-6
MASK_VALUE = -1e30

LANES = 128
SUBLANES = 8
VMEM_LIMIT_BYTES = 56 * 1024 * 1024

DIL_PATTERNS = ((128, 1), (512, 4), (2048, 16))
HEADS_PER_GROUP = 8
HEAD_DIM_A = 64
ATTN_WIDTH = HEADS_PER_GROUP * HEAD_DIM_A
BAND_RADIUS = 64
CONV_WIDTH = 31
CONV_HALO = 16
FOURIER_GROUP_DIM = 256
GLA_HEADS = 4
GLA_HEAD_K = 128
GLA_HEAD_V = 256
GLA_DK = GLA_HEADS * GLA_HEAD_K
GLA_DV = GLA_HEADS * GLA_HEAD_V
GLA_TAU = 16.0
GLA_CHUNK = 64

NT_DIMS = (((1,), (1,)), ((), ()))
TN_DIMS = (((0,), (0,)), ((), ()))


def _params(*semantics):
    return pltpu.CompilerParams(dimension_semantics=semantics, vmem_limit_bytes=VMEM_LIMIT_BYTES)


def _resident(shape):
    nd = len(shape)
    return pl.BlockSpec(shape, lambda *_: (0,) * nd, pipeline_mode=pl.Buffered(1))


def _rms(x, g):
    ms = jnp.mean(x * x, axis=-1, keepdims=True)
    return x * lax.rsqrt(ms + EPS) * g


def _silu(x):
    return x * jax.nn.sigmoid(x)


def _dot(a, b):
    return jnp.dot(a, b, preferred_element_type=F32)


def _ffn_blocks(block_input, x_ref, g_ref, w1_ref, w3_ref, w2_ref, gf_ref, o_ref, gate_ref, *, ff_chunk, n_sub, final,
                pieces_per_chunk=1):
    tm, d = o_ref.shape
    d_ff = w1_ref.shape[1]
    rows = [slice(i * (tm // n_sub), (i + 1) * (tm // n_sub)) for i in range(n_sub)]
    stages = [block_input(i, rs) if block_input is not None else iter(()) for i, rs in enumerate(rows)]
    for _ in stages[0]:
        pass
    for i, rs in enumerate(rows):
        ahead = stages[i + 1] if i + 1 < n_sub else iter(())
        hn = _rms(x_ref[rs, :], g_ref[...]).astype(BF16)
        for c0 in range(0, d_ff, ff_chunk):
            sl = slice(c0, c0 + ff_chunk)
            a = _dot(hn, w1_ref[:, sl])
            b = _dot(hn, w3_ref[:, sl])
            gate_ref[rs, sl] = (_silu(a) * b).astype(BF16)
            for _ in range(pieces_per_chunk):
                next(ahead, None)
        for _ in ahead:
            pass
    for rs in rows:
        if final:
            y = x_ref[rs, :] + 0.5 * _dot(gate_ref[rs, :], w2_ref[...])
            o_ref[rs, :] = _rms(y, gf_ref[...])
        else:
            for c0 in range(0, d, ff_chunk):
                sl = slice(c0, c0 + ff_chunk)
                o_ref[rs, sl] = x_ref[rs, sl] + 0.5 * _dot(gate_ref[rs, :], w2_ref[:, sl])


def _ffn_body(x_ref, g_ref, w1_ref, w3_ref, w2_ref, gf_ref, o_ref, gate_ref, *, ff_chunk, n_sub, final):
    _ffn_blocks(None, x_ref, g_ref, w1_ref, w3_ref, w2_ref, gf_ref, o_ref, gate_ref,
                ff_chunk=ff_chunk, n_sub=n_sub, final=final)


def _mixer_ffn_body(*refs, n_in, block_input, ff_chunk, n_sub, final, pieces_per_chunk):
    ins = refs[:n_in]
    g_ref, w1_ref, w3_ref, w2_ref, gf_ref, o_ref, x_ref, gate_ref = refs[n_in:n_in + 8]
    scratch = refs[n_in + 8:]
    _ffn_blocks(lambda i, rs: block_input(i, rs, ins, scratch, x_ref), x_ref, g_ref, w1_ref, w3_ref, w2_ref,
                gf_ref, o_ref, gate_ref, ff_chunk=ff_chunk, n_sub=n_sub, final=final,
                pieces_per_chunk=pieces_per_chunk)


def _mixer_ffn(block_input, inputs, in_specs, scratch_shapes, ffn_params, *, grid, out_spec, out_shape, tm, final,
               name, ff_chunk=256, n_sub=2, pieces_per_chunk=1):
    g, w1, w3, w2, gf = ffn_params
    d, d_ff = w1.shape
    return pl.pallas_call(
        functools.partial(_mixer_ffn_body, n_in=len(inputs), block_input=block_input, ff_chunk=ff_chunk,
                          n_sub=n_sub, final=final, pieces_per_chunk=pieces_per_chunk),
        grid=grid,
        in_specs=list(in_specs) + [_resident((1, d)), _resident((d, d_ff)), _resident((d, d_ff)),
                                   _resident((d_ff, d)), _resident((1, d))],
        out_specs=out_spec,
        out_shape=out_shape,
        scratch_shapes=[pltpu.VMEM((tm, d), F32), pltpu.VMEM((tm, d_ff), BF16)] + list(scratch_shapes),
        compiler_params=_params(*(("parallel",) * len(grid))),
        name=name,
    )(*inputs, g, w1, w3, w2, gf)


def _ffn(h2, g, w1, w3, w2, gf, *, final, tm=1024, ff_chunk=256, n_sub=4):
    t, d = h2.shape
    d_ff = w1.shape[1]
    row = pl.BlockSpec((tm, d), lambda i: (i, 0))
    return pl.pallas_call(
        functools.partial(_ffn_body, ff_chunk=ff_chunk, n_sub=n_sub, final=final),
        grid=(t // tm,),
        in_specs=[row, _resident((1, d)), _resident((d, d_ff)), _resident((d, d_ff)),
                  _resident((d_ff, d)), _resident((1, d))],
        out_specs=row,
        out_shape=jax.ShapeDtypeStruct((t, d), F32),
        scratch_shapes=[pltpu.VMEM((tm, d_ff), BF16)],
        compiler_params=_params("parallel"),
        name="ffn_final" if final else "ffn",
    )(h2, g, w1, w3, w2, gf)


def _attn_proj_body(x_ref, g_ref, w_ref, o1_ref, o4_ref, o16_ref, y_ref, *, tm, n_sub):
    n = 3 * ATTN_WIDTH
    o_refs = (o1_ref, o4_ref, o16_ref)
    blk = tm // n_sub
    normed = {}

    def project(i, grp):
        if i not in normed:
            normed[i] = _rms(x_ref[i * blk:(i + 1) * blk, :], g_ref[...]).astype(BF16)
        return _dot(normed[i], w_ref[:, grp * n:(grp + 1) * n])

    def emit(i, grp, y):
        o_ref, dil = o_refs[grp], DIL_PATTERNS[grp][1]
        out_rows = slice(i * (blk // dil), (i + 1) * (blk // dil))
        if dil == 1:
            o_ref[0, out_rows, :] = y.astype(BF16)
            return
        for c in range(n // LANES):
            y_ref[c] = y[:, c * LANES:(c + 1) * LANES]
        for r in range(dil):
            for c in range(n // LANES):
                o_ref[r, out_rows, c * LANES:(c + 1) * LANES] = (
                    y_ref[c, pl.ds(r, blk // dil, stride=dil), :].astype(BF16))

    work = [(i, grp) for i in range(n_sub) for grp in range(len(DIL_PATTERNS))]
    y = project(*work[0])
    for j, item in enumerate(work):
        y_next = project(*work[j + 1]) if j + 1 < len(work) else None
        emit(*item, y)
        y = y_next


def _attn_proj(h, g, w, tm=512, n_sub=1):
    b, s, d = h.shape
    n = 3 * ATTN_WIDTH
    out_specs, out_shape = [], []
    for _, dil in DIL_PATTERNS:
        out_specs.append(pl.BlockSpec((None, dil, tm // dil, n), lambda bi, i: (bi, 0, i, 0)))
        out_shape.append(jax.ShapeDtypeStruct((b, dil, s // dil, n), BF16))
    return pl.pallas_call(
        functools.partial(_attn_proj_body, tm=tm, n_sub=n_sub),
        grid=(b, s // tm),
        in_specs=[pl.BlockSpec((None, tm, d), lambda bi, i: (bi, i, 0)), _resident((1, d)),
                  _resident((d, len(DIL_PATTERNS) * n))],
        out_specs=out_specs,
        out_shape=out_shape,
        scratch_shapes=[pltpu.VMEM((n // LANES, tm // n_sub, LANES), F32)],
        compiler_params=_params("parallel", "parallel"),
        name="attn_proj",
    )(h, g, w)


def _band_body(q_ref, kp_ref, km_ref, kn_ref, vp_ref, vm_ref, vn_ref, o_ref, lse_ref, *, tq, sub, slopes):
    i = pl.program_id(2)
    rad = BAND_RADIUS
    nk = 3 * rad
    n_sb = tq // rad
    r = lax.broadcasted_iota(jnp.int32, (rad, nk), 0)
    c = lax.broadcasted_iota(jnp.int32, (rad, nk), 1)
    dist = jnp.abs(c - rad - r)
    absrel = dist.astype(F32)
    out_of_band = jnp.where(dist <= rad, 0.0, -MASK_VALUE)
    col = lax.broadcasted_iota(jnp.int32, (1, nk), 1)
    is_first = (i == 0).astype(F32)
    is_last = ((i + 1) * tq >= sub).astype(F32)
    before_start = jnp.where(col < rad, -MASK_VALUE, 0.0) * is_first
    after_end = jnp.where(col >= 2 * rad, -MASK_VALUE, 0.0) * is_last
    lane = lax.broadcasted_iota(jnp.int32, (rad, LANES), 1)
    low = lane < HEAD_DIM_A
    lowf = low.astype(F32)
    n_pairs = HEADS_PER_GROUP // 2

    def scores(p):
        sl = slice(LANES * p, LANES * (p + 1))
        kc = jnp.concatenate([kp_ref[:, sl], km_ref[:, sl], kn_ref[:, sl]], axis=0)
        pen = jnp.concatenate([slopes[2 * p + hh] * absrel + out_of_band for hh in range(2)], axis=0)
        out = []
        for sb in range(n_sb):
            q2 = q_ref[sb * rad:(sb + 1) * rad, sl].astype(F32)
            qm = jnp.concatenate([q2 * lowf, q2 * (1.0 - lowf)], axis=0).astype(BF16)
            sc = lax.dot_general(qm, kc[sb * rad:sb * rad + nk], NT_DIMS, preferred_element_type=F32) - pen
            if sb == 0:
                sc = sc - before_start
            if sb == n_sb - 1:
                sc = sc - after_end
            out.append(sc)
        return out

    def finish(p, scs):
        sl = slice(LANES * p, LANES * (p + 1))
        vc = jnp.concatenate([vp_ref[:, sl], vm_ref[:, sl], vn_ref[:, sl]], axis=0)
        for sb, sc in enumerate(scs):
            rows = slice(sb * rad, (sb + 1) * rad)
            m = jnp.max(sc, axis=-1, keepdims=True)
            e = jnp.exp(sc - m)
            l = jnp.sum(e, axis=-1, keepdims=True)
            o = _dot(e.astype(BF16), vc[sb * rad:sb * rad + nk]) / l
            lse = m + jnp.log(l)
            o_ref[rows, sl] = jnp.where(low, o[:rad], o[rad:]).astype(BF16)
            rest = 0.0 if p == 0 else lse_ref[rows, :]
            lse_ref[rows, :] = jnp.where(lane == 2 * p, lse[:rad], jnp.where(lane == 2 * p + 1, lse[rad:], rest))

    pending = scores(0)
    for p in range(1, n_pairs):
        upcoming = scores(p)
        finish(p - 1, pending)
        pending = upcoming
    finish(n_pairs - 1, pending)


def _band_attention(qkv, group, dil, tq=512):
    b, _, sub, _ = qkv.shape
    w = ATTN_WIDTH
    tq = min(tq, sub)
    hb = tq // BAND_RADIUS
    n_halo = sub // BAND_RADIUS
    n_heads = len(DIL_PATTERNS) * HEADS_PER_GROUP
    slopes = tuple(float(2.0 ** (-8.0 * (group * HEADS_PER_GROUP + hd + 1) / n_heads)) * dil
                   for hd in range(HEADS_PER_GROUP))

    def main(col):
        return pl.BlockSpec((None, None, tq, w), lambda bi, r, i: (bi, r, i, col))

    def prev(col):
        return pl.BlockSpec((None, None, BAND_RADIUS, w),
                            lambda bi, r, i: (bi, r, jnp.maximum(i * hb - 1, 0), col))

    def nxt(col):
        return pl.BlockSpec((None, None, BAND_RADIUS, w),
                            lambda bi, r, i: (bi, r, jnp.minimum((i + 1) * hb, n_halo - 1), col))

    return pl.pallas_call(
        functools.partial(_band_body, tq=tq, sub=sub, slopes=slopes),
        grid=(b, dil, sub // tq),
        in_specs=[main(0), prev(1), main(1), nxt(1), prev(2), main(2), nxt(2)],
        out_specs=[main(0), pl.BlockSpec((None, None, tq, LANES), lambda bi, r, i: (bi, r, i, 0))],
        out_shape=[jax.ShapeDtypeStruct((b, dil, sub, w), BF16), jax.ShapeDtypeStruct((b, dil, sub, LANES), F32)],
        compiler_params=_params("parallel", "parallel", "parallel"),
        name=f"band_attn_d{dil}",
    )(qkv, qkv, qkv, qkv, qkv, qkv, qkv)


def _attn_out_block(i, rs, ins, scratch, x_ref):
    o1, o4, o16, l1, l4, l16, h_ref, w_ref = ins
    os_ref, ls_ref = scratch
    blk = rs.stop - rs.start
    n_lane_groups = ATTN_WIDTH // LANES
    for grp, (o_ref, l_ref, (_, dil)) in enumerate(zip((o1, o4, o16), (l1, l4, l16), DIL_PATTERNS)):
        src = slice(i * (blk // dil), (i + 1) * (blk // dil))
        for r in range(dil):
            rows = pl.ds(r, blk // dil, stride=dil) if dil > 1 else slice(None)
            ls_ref[grp, rows, :] = l_ref[r, src, :]
            for c in range(n_lane_groups):
                os_ref[grp, c, rows, :] = o_ref[r, src, c * LANES:(c + 1) * LANES].astype(F32)
        yield
    ls = [ls_ref[grp] for grp in range(len(DIL_PATTERNS))]
    m = jnp.maximum(jnp.maximum(ls[0], ls[1]), ls[2])
    es = [jnp.exp(l - m) for l in ls]
    inv = 1.0 / (es[0] + es[1] + es[2])
    row = lax.broadcasted_iota(jnp.int32, (LANES, ATTN_WIDTH), 0)
    col = lax.broadcasted_iota(jnp.int32, (LANES, ATTN_WIDTH), 1)
    spread = jnp.where(col // HEAD_DIM_A == row, 1.0, 0.0).astype(BF16)
    yield
    o = 0.0
    for grp, e in enumerate(es):
        wgt = e * inv
        hi = wgt.astype(BF16)
        lo = (wgt - hi.astype(F32)).astype(BF16)
        full = _dot(hi, spread) + _dot(lo, spread)
        o = o + full * jnp.concatenate([os_ref[grp, c] for c in range(n_lane_groups)], axis=1)
        yield
    x_ref[rs, :] = h_ref[rs, :] + _dot(o.astype(BF16), w_ref[...])


def _attn_out_ffn(os_, lses, h, w_o, ffn_params, *, final, tm=512, n_sub=2):
    b, s, d = h.shape
    w = ATTN_WIDTH
    specs = [pl.BlockSpec((None, dil, tm // dil, w), lambda bi, i: (bi, 0, i, 0)) for _, dil in DIL_PATTERNS]
    lspecs = [pl.BlockSpec((None, dil, tm // dil, LANES), lambda bi, i: (bi, 0, i, 0)) for _, dil in DIL_PATTERNS]
    row = pl.BlockSpec((None, tm, d), lambda bi, i: (bi, i, 0))
    n_grp = len(DIL_PATTERNS)
    blk = tm // n_sub
    return _mixer_ffn(
        _attn_out_block, (*os_, *lses, h, w_o), specs + lspecs + [row, _resident((w, d))],
        [pltpu.VMEM((n_grp, w // LANES, blk, LANES), F32), pltpu.VMEM((n_grp, blk, LANES), F32)],
        ffn_params, grid=(b, s // tm), out_spec=row, out_shape=jax.ShapeDtypeStruct((b, s, d), F32),
        tm=tm, final=final, name="attn_out_ffn", n_sub=n_sub)


def _attention_mixer_ffn(h, g, w_qkv, w_o, ffn_params, *, final):
    w = ATTN_WIDTH
    col = jnp.arange(w_qkv.shape[1]) % (3 * w)
    wq = (w_qkv * jnp.where(col < w, HEAD_DIM_A ** -0.5, 1.0)).astype(BF16)
    qkvs = _attn_proj(h, g, wq)
    os_, lses = [], []
    for grp, (_, dil) in enumerate(DIL_PATTERNS):
        o, lse = _band_attention(qkvs[grp], grp, dil)
        os_.append(o)
        lses.append(lse)
    return _attn_out_ffn(os_, lses, h, w_o.astype(BF16), ffn_params, final=final)


def _conv_block(i, rs, ins, scratch, x_ref, *, n_sub, n_tiles, row_chunk, lane_chunk):
    xp_ref, xm_ref, xn_ref, g_ref, w1_ref, b1_ref, wdw_ref, bdw_ref, lng_ref, lnb_ref, w2_ref, b2_ref = ins
    xext_ref, zext_ref, zs_ref, cv_ref = scratch
    t = pl.program_id(1)
    d = xm_ref.shape[1]
    halo = CONV_HALO
    blk = rs.stop - rs.start
    ext = blk + 2 * halo
    xext_ref[0:halo, :] = xp_ref[...] if i == 0 else xm_ref[rs.start - halo:rs.start, :]
    xext_ref[halo:halo + blk, :] = xm_ref[rs, :]
    xext_ref[halo + blk:ext, :] = xn_ref[...] if i == n_sub - 1 else xm_ref[rs.stop:rs.stop + halo, :]
    u = _rms(xext_ref[...], g_ref[...]).astype(BF16)
    for c0 in range(0, d, lane_chunk):
        cs = slice(c0, c0 + lane_chunk)
        za = _dot(u, w1_ref[:, cs]) + b1_ref[:, cs]
        zg = _dot(u, w1_ref[:, d + c0:d + c0 + lane_chunk]) + b1_ref[:, d + c0:d + c0 + lane_chunk]
        zext_ref[:, cs] = za * jax.nn.sigmoid(zg)
        yield
    if i == 0:
        zext_ref[0:halo, :] = jnp.where(t > 0, zext_ref[0:halo, :], 0.0)
    if i == n_sub - 1:
        zext_ref[halo + blk:ext, :] = jnp.where(t < n_tiles - 1, zext_ref[halo + blk:ext, :], 0.0)
    span = ext - SUBLANES
    for r in range(1, SUBLANES):
        for c0 in range(0, d, LANES):
            zs_ref[r - 1, :, c0:c0 + LANES] = zext_ref[r:r + span, c0:c0 + LANES]
        yield
    first = halo - CONV_WIDTH // 2
    for c0 in range(0, d, lane_chunk):
        cs = slice(c0, c0 + lane_chunk)
        for r0 in range(0, blk, row_chunk):
            acc = jnp.zeros((row_chunk, lane_chunk), F32)
            for k in range(CONV_WIDTH):
                q, r = divmod(first + k, SUBLANES)
                rows = slice(r0 + q * SUBLANES, r0 + q * SUBLANES + row_chunk)
                tap = zext_ref[rows, cs] if r == 0 else zs_ref[r - 1, rows, cs]
                acc = acc + wdw_ref[k:k + 1, cs] * tap
            cv_ref[r0:r0 + row_chunk, cs] = acc + bdw_ref[:, cs]
            yield
    z = cv_ref[...]
    mu = jnp.mean(z, axis=-1, keepdims=True)
    zc = z - mu
    var = jnp.mean(zc * zc, axis=-1, keepdims=True)
    zn = zc * lax.rsqrt(var + EPS) * lng_ref[...] + lnb_ref[...]
    y = _silu(zn).astype(BF16)
    yield
    x_ref[rs, :] = xm_ref[rs, :] + _dot(y, w2_ref[...]) + b2_ref[...]


def _conv_body(*refs, n_in, n_sub, n_tiles, glu_pieces):
    ins = refs[:n_in]
    o_ref = refs[n_in]
    scratch = refs[n_in + 1:]
    per_block = len(scratch) // n_sub
    blk = o_ref.shape[0] // n_sub
    stages = [_conv_block(i, slice(i * blk, (i + 1) * blk), ins, scratch[i * per_block:(i + 1) * per_block], o_ref,
                          n_sub=n_sub, n_tiles=n_tiles, row_chunk=128, lane_chunk=256) for i in range(n_sub)]
    for stage in stages:
        for _ in range(glu_pieces):
            next(stage)
    for stage in stages:
        for _ in stage:
            pass


def _conv_mixer(h, g, w_pw1, b_pw1, w_dw, b_dw, ln_g, ln_b, w_pw2, b_pw2, tm=512, n_sub=1):
    b, s, d = h.shape
    n_tiles = s // tm
    hb = tm // CONV_HALO
    n_halo = s // CONV_HALO
    blk = tm // n_sub
    ext = blk + 2 * CONV_HALO
    main = pl.BlockSpec((None, tm, d), lambda bi, i: (bi, i, 0))
    prev = pl.BlockSpec((None, CONV_HALO, d), lambda bi, i: (bi, jnp.maximum(i * hb - 1, 0), 0))
    nxt = pl.BlockSpec((None, CONV_HALO, d), lambda bi, i: (bi, jnp.minimum((i + 1) * hb, n_halo - 1), 0))
    vec = _resident((1, d))
    inputs = (h, h, h, g, w_pw1.astype(BF16), b_pw1.reshape(1, 2 * d), w_dw, b_dw.reshape(1, d),
              ln_g.reshape(1, d), ln_b.reshape(1, d), w_pw2.astype(BF16), b_pw2.reshape(1, d))
    block_scratch = [pltpu.VMEM((ext, d), F32), pltpu.VMEM((ext, d), F32),
                     pltpu.VMEM((SUBLANES - 1, ext - SUBLANES, d), F32), pltpu.VMEM((blk, d), F32)]
    return pl.pallas_call(
        functools.partial(_conv_body, n_in=len(inputs), n_sub=n_sub, n_tiles=n_tiles, glu_pieces=d // 256),
        grid=(b, n_tiles),
        in_specs=[prev, main, nxt, vec, _resident((d, 2 * d)), _resident((1, 2 * d)), _resident((CONV_WIDTH, d)),
                  vec, vec, vec, _resident((d, d)), vec],
        out_specs=main,
        out_shape=jax.ShapeDtypeStruct((b, s, d), F32),
        scratch_shapes=block_scratch * n_sub,
        compiler_params=_params("parallel", "parallel"),
        name="conv_module",
    )(*inputs)


def _pack_complex(re, im):
    hi = lax.bitcast_convert_type(re.astype(BF16).astype(F32), jnp.uint32)
    lo = lax.bitcast_convert_type(im.astype(BF16).astype(F32), jnp.uint32)
    return hi | (lo >> 16)


def _unpack_complex(word):
    re = lax.bitcast_convert_type(word & jnp.uint32(0xFFFF0000), F32)
    im = lax.bitcast_convert_type(word << 16, F32)
    return re.astype(BF16), im.astype(BF16)


def _fft1_body(x_ref, g_ref, w1_ref, ct_ref, st_ref, y_ref):
    n1, n2s, d = x_ref.shape
    rows = n1 * n2s
    u = _rms(x_ref[...].reshape(rows, d), g_ref[...]).astype(BF16)
    y = _dot(w1_ref[...], u)
    yr, ys = y[:rows], y[rows:]
    ct = jnp.concatenate([ct_ref[...].reshape(rows, LANES)] * (d // LANES), axis=1)
    st = jnp.concatenate([st_ref[...].reshape(rows, LANES)] * (d // LANES), axis=1)
    y_ref[...] = _pack_complex(yr * ct - ys * st, -(yr * st + ys * ct)).reshape(n1, n2s, d)


def _fft2_body(y_ref, h_ref, m2_ref, cc_ref, sc_ref, wf_ref, bf_ref, o_ref, zr_ref, zi_ref, mix_ref, res_ref, *,
               out_rows):
    k1s, n2, d = y_ref.shape
    for jj in range(k1s):
        re, im = _unpack_complex(y_ref[jj])
        z = _dot(m2_ref[...], jnp.concatenate([re, im], axis=0))
        zr_ref[jj * n2:(jj + 1) * n2, :] = z[:n2].astype(BF16)
        zi_ref[jj * n2:(jj + 1) * n2, :] = z[n2:].astype(BF16)
    gd = FOURIER_GROUP_DIM
    for c0 in range(0, d, gd):
        cs = slice(c0, c0 + gd)
        mix_ref[:, cs] = (_dot(zr_ref[:, cs], cc_ref[...]) + _dot(zi_ref[:, cs], sc_ref[...])).astype(BF16)
    for r0 in range(0, k1s * n2, out_rows):
        res_ref[r0:r0 + out_rows, :] = _dot(mix_ref[r0:r0 + out_rows, :], wf_ref[...]) + bf_ref[...]
    o_ref[...] = h_ref[...] + jnp.swapaxes(res_ref[...].reshape(k1s, n2, d), 0, 1)


def _dft_tables(s, n2_per_step):
    n2 = 128
    n1 = s // n2
    def cs(n):
        ang = 2.0 * np.pi * (np.outer(np.arange(n), np.arange(n)) % n) / n
        return np.cos(ang), np.sin(ang)
    c1, s1 = cs(n1)
    c2, s2 = cs(n2)
    cc, sc = cs(FOURIER_GROUP_DIM)
    eye = np.eye(n2_per_step)
    scale = 1.0 / math.sqrt(s * FOURIER_GROUP_DIM)
    ang_t = 2.0 * np.pi * (np.outer(np.arange(n1), np.arange(n2)) % s) / s
    return dict(
        n1=n1, n2=n2,
        w1=jnp.asarray(np.concatenate([np.kron(c1, eye), np.kron(s1, eye)], axis=0), F32).astype(BF16),
        m2=jnp.asarray(np.block([[c2, s2], [-s2, c2]]), F32).astype(BF16),
        cc=jnp.asarray(cc * scale, F32).astype(BF16),
        sc=jnp.asarray(sc * scale, F32).astype(BF16),
        ct=jnp.asarray(np.cos(ang_t), F32), st=jnp.asarray(np.sin(ang_t), F32))


def _fourier_mixer(h, g, w_f, b_f, n2_per_step=SUBLANES, k1_per_step=SUBLANES, out_rows=256):
    b, s, d = h.shape
    tb = _dft_tables(s, n2_per_step)
    n1, n2 = tb["n1"], tb["n2"]
    ct = jnp.broadcast_to(tb["ct"][:, :, None], (n1, n2, LANES))
    st = jnp.broadcast_to(tb["st"][:, :, None], (n1, n2, LANES))
    rows1 = n1 * n2_per_step
    xblk = pl.BlockSpec((None, n1, n2_per_step, d), lambda bi, j: (bi, 0, j, 0))
    tw = pl.BlockSpec((n1, n2_per_step, LANES), lambda bi, j: (0, j, 0))
    y = pl.pallas_call(
        _fft1_body,
        grid=(b, n2 // n2_per_step),
        in_specs=[xblk, _resident((1, d)), _resident((2 * rows1, rows1)), tw, tw],
        out_specs=xblk,
        out_shape=jax.ShapeDtypeStruct((b, n1, n2, d), jnp.uint32),
        compiler_params=_params("parallel", "parallel"),
        name="fft_stage1",
    )(h.reshape(b, n1, n2, d), g, tb["w1"], ct, st)
    gd = FOURIER_GROUP_DIM
    rows2 = k1_per_step * n2
    oblk = pl.BlockSpec((None, n2, k1_per_step, d), lambda bi, k: (bi, 0, k, 0))
    out = pl.pallas_call(
        functools.partial(_fft2_body, out_rows=out_rows),
        grid=(b, n1 // k1_per_step),
        in_specs=[pl.BlockSpec((None, k1_per_step, n2, d), lambda bi, k: (bi, k, 0, 0)), oblk,
                  _resident((2 * n2, 2 * n2)), _resident((gd, gd)), _resident((gd, gd)),
                  _resident((d, d)), _resident((1, d))],
        out_specs=oblk,
        out_shape=jax.ShapeDtypeStruct((b, n2, n1, d), F32),
        scratch_shapes=[pltpu.VMEM((rows2, d), BF16)] * 3 + [pltpu.VMEM((rows2, d), F32)],
        compiler_params=_params("parallel", "parallel"),
        name="fft_stage2",
    )(y, h.reshape(b, n2, n1, d), tb["m2"], tb["cc"], tb["sc"], w_f.astype(BF16), b_f.reshape(1, d))
    return out.reshape(b, s, d)


def _gla_proj_body(x_ref, g_ref, win_ref, wa1_ref, wa2_ref, ba_ref, q_ref, k_ref, v_ref, r_ref, la_ref, *, n_sub):
    tm = x_ref.shape[0]
    dk = q_ref.shape[1]
    dv = v_ref.shape[1]
    rows = [slice(i * (tm // n_sub), (i + 1) * (tm // n_sub)) for i in range(n_sub)]
    us = [_rms(x_ref[rs, :], g_ref[...]).astype(BF16) for rs in rows]
    lows = [_dot(u, wa1_ref[...]).astype(BF16) for u in us]
    for rs, low in zip(rows, lows):
        z = _dot(low, wa2_ref[...]) + ba_ref[...]
        log_sig = jnp.minimum(z, 0.0) - jnp.log(1.0 + jnp.exp(-jnp.abs(z)))
        la_ref[rs, :] = log_sig / GLA_TAU
    for rs, u in zip(rows, us):
        proj = _dot(u, win_ref[...])
        q_ref[rs, :] = (proj[:, :dk] * (GLA_HEAD_K ** -0.5)).astype(BF16)
        k_ref[rs, :] = proj[:, dk:2 * dk].astype(BF16)
        v_ref[rs, :] = proj[:, 2 * dk:2 * dk + dv].astype(BF16)
        r_ref[rs, :] = proj[:, 2 * dk + dv:].astype(BF16)


def _gla_proj(h2, g, w_in, wa1, wa2, ba, tm=512, n_sub=2):
    t, d = h2.shape
    row = lambda n: pl.BlockSpec((tm, n), lambda i: (i, 0))
    rank2 = wa1.shape[1]
    return pl.pallas_call(
        functools.partial(_gla_proj_body, n_sub=n_sub),
        grid=(t // tm,),
        in_specs=[row(d), _resident((1, d)), _resident(w_in.shape), _resident((d, rank2)),
                  _resident((rank2, 2 * GLA_DK)), _resident((1, 2 * GLA_DK))],
        out_specs=[row(GLA_DK), row(GLA_DK), row(GLA_DV), row(GLA_DV), row(2 * GLA_DK)],
        out_shape=[jax.ShapeDtypeStruct((t, GLA_DK), BF16), jax.ShapeDtypeStruct((t, GLA_DK), BF16),
                   jax.ShapeDtypeStruct((t, GLA_DV), BF16), jax.ShapeDtypeStruct((t, GLA_DV), BF16),
                   jax.ShapeDtypeStruct((t, 2 * GLA_DK), F32)],
        compiler_params=_params("parallel"),
        name="gla_proj",
    )(h2, g, w_in, wa1, wa2, ba)


def _gla_prepare(q_ref, k_ref, la_ref, *, backward, n_chunks):
    c = GLA_CHUNK
    rows = n_chunks * c
    ri = lax.broadcasted_iota(jnp.int32, (rows, rows), 0)
    ci = lax.broadcasted_iota(jnp.int32, (rows, rows), 1)
    same_chunk = (ri // c) == (ci // c)
    if backward:
        tri = same_chunk & (ci >= ri)
        keep = same_chunk & (ci > ri)
    else:
        tri = same_chunk & (ci <= ri)
        keep = tri
    tri = jnp.where(tri, 1.0, 0.0).astype(BF16)
    la = la_ref[...]
    la_hi = la.astype(BF16)
    la_lo = (la - la_hi.astype(F32)).astype(BF16)
    cum = _dot(tri, la_hi) + _dot(tri, la_lo)
    ref_row = c // 2 if backward else c // 2 - 1
    last_row = 0 if backward else c - 1
    lasts = [cum[a * c + last_row:a * c + last_row + 1] for a in range(n_chunks)]
    ref = jnp.concatenate([jnp.broadcast_to(cum[a * c + ref_row:a * c + ref_row + 1], (c, cum.shape[1]))
                           for a in range(n_chunks)], axis=0)
    last = jnp.concatenate([jnp.broadcast_to(l, (c, cum.shape[1])) for l in lasts], axis=0)
    q = q_ref[...].astype(F32)
    k = k_ref[...].astype(F32)
    return dict(
        keep=keep, lasts=lasts, order=range(n_chunks - 1, -1, -1) if backward else range(n_chunks),
        q_in=(q * jnp.exp(cum - ref)).astype(BF16), k_in=(k * jnp.exp(ref - cum)).astype(BF16),
        q_x=(q * jnp.exp(cum)).astype(BF16), k_x=(k * jnp.exp(last - cum)).astype(BF16))


def _gla_scan_body(qf, kf, vf, laf, qb, kb, vb, lab, of_ref, ob_ref, stf_ref, stb_ref, accf_ref, accb_ref, *,
                   chunks_per_step):
    @pl.when(pl.program_id(1) == 0)
    def _():
        stf_ref[...] = jnp.zeros_like(stf_ref)
        stb_ref[...] = jnp.zeros_like(stb_ref)
    c = GLA_CHUNK
    heads = [(slice(GLA_HEAD_K * hd, GLA_HEAD_K * (hd + 1)), slice(GLA_HEAD_V * hd, GLA_HEAD_V * (hd + 1)))
             for hd in range(GLA_HEADS)]
    dirs = []
    for bb in range(qf.shape[0]):
        dirs.append((_gla_prepare(qf.at[bb], kf.at[bb], laf.at[bb], backward=False, n_chunks=chunks_per_step),
                     vf.at[bb], of_ref.at[bb], stf_ref.at[bb], accf_ref.at[bb]))
        dirs.append((_gla_prepare(qb.at[bb], kb.at[bb], lab.at[bb], backward=True, n_chunks=chunks_per_step),
                     vb.at[bb], ob_ref.at[bb], stb_ref.at[bb], accb_ref.at[bb]))
    scores = [[lax.dot_general(p["q_in"][:, ks], p["k_in"][:, ks], NT_DIMS, preferred_element_type=F32)
               for ks, _ in heads] for p, *_ in dirs]
    for (p, v_ref, _, _, acc_ref), scs in zip(dirs, scores):
        for (_, vs), sc in zip(heads, scs):
            acc_ref[:, vs] = _dot(jnp.where(p["keep"], sc, 0.0).astype(BF16), v_ref[:, vs])
    states = [[st_ref[hd] for hd in range(GLA_HEADS)] for _, _, _, st_ref, _ in dirs]
    for step in range(chunks_per_step):
        for di, (p, v_ref, o_ref, _, acc_ref) in enumerate(dirs):
            a = p["order"][step]
            rs = slice(a * c, (a + 1) * c)
            for hd, (ks, vs) in enumerate(heads):
                state_t = states[di][hd]
                inter = lax.dot_general(p["q_x"][rs, ks], state_t.astype(BF16), NT_DIMS, preferred_element_type=F32)
                o_ref[rs, vs] = (acc_ref[rs, vs] + inter).astype(o_ref.dtype)
                states[di][hd] = state_t * jnp.exp(p["lasts"][a][:, ks]) + lax.dot_general(
                    v_ref[rs, vs], p["k_x"][rs, ks], TN_DIMS, preferred_element_type=F32)
    for di, (_, _, _, st_ref, _) in enumerate(dirs):
        for hd in range(GLA_HEADS):
            st_ref[hd] = states[di][hd]


def _gla_scan(q, k, v, la, b, s, chunks_per_step=4, seqs_per_step=2):
    rows = chunks_per_step * GLA_CHUNK
    nb = s // rows
    nq = seqs_per_step
    fwd = lambda n, col=0: pl.BlockSpec((nq, rows, n), lambda bi, i: (bi, i, col))
    bwd = lambda n, col=0: pl.BlockSpec((nq, rows, n), lambda bi, i: (bi, nb - 1 - i, col))
    q3, k3, v3 = (a.reshape(b, s, a.shape[-1]) for a in (q, k, v))
    la3 = la.reshape(b, s, 2 * GLA_DK)
    state = pltpu.VMEM((nq, GLA_HEADS, GLA_HEAD_V, GLA_HEAD_K), F32)
    acc = pltpu.VMEM((nq, rows, GLA_DV), F32)
    return pl.pallas_call(
        functools.partial(_gla_scan_body, chunks_per_step=chunks_per_step),
        grid=(b // nq, nb),
        in_specs=[fwd(GLA_DK), fwd(GLA_DK), fwd(GLA_DV), fwd(GLA_DK, 0),
                  bwd(GLA_DK), bwd(GLA_DK), bwd(GLA_DV), bwd(GLA_DK, 1)],
        out_specs=[fwd(GLA_DV), bwd(GLA_DV)],
        out_shape=[jax.ShapeDtypeStruct((b, s, GLA_DV), BF16)] * 2,
        scratch_shapes=[state, state, acc, acc],
        compiler_params=_params("parallel", "arbitrary"),
        name="gla_scan",
    )(q3, k3, v3, la3, q3, k3, v3, la3)


def _gla_out_block(i, rs, ins, scratch, x_ref):
    of_ref, ob_ref, r_ref, ng_ref, h_ref, w_ref = ins
    parts = []
    for hd in range(GLA_HEADS):
        vs = slice(GLA_HEAD_V * hd, GLA_HEAD_V * (hd + 1))
        oh = of_ref[rs, vs].astype(F32) + ob_ref[rs, vs].astype(F32)
        oh = oh * lax.rsqrt(jnp.mean(oh * oh, axis=-1, keepdims=True) + EPS)
        parts.append((oh * ng_ref[:, vs] * _silu(r_ref[rs, vs].astype(F32))).astype(BF16))
        yield
    x_ref[rs, :] = h_ref[rs, :] + _dot(jnp.concatenate(parts, axis=1), w_ref[...])


def _gla_out_ffn(o_f, o_b, r, norm_g, h2, w_o, ffn_params, *, final, tm=512, n_sub=2):
    t, d = h2.shape
    row = lambda n: pl.BlockSpec((tm, n), lambda i: (i, 0))
    return _mixer_ffn(
        _gla_out_block, (o_f.reshape(t, GLA_DV), o_b.reshape(t, GLA_DV), r, norm_g, h2, w_o),
        [row(GLA_DV), row(GLA_DV), row(GLA_DV), _resident((1, GLA_DV)), row(d), _resident((GLA_DV, d))], [],
        ffn_params, grid=(t // tm,), out_spec=row(d), out_shape=jax.ShapeDtypeStruct((t, d), F32),
        tm=tm, final=final, name="gla_out_ffn", n_sub=n_sub)


def _gla_mixer_ffn(h, g, w_in, w_a1, w_a2, b_a, norm_g, w_o, ffn_params, *, final):
    b, s, d = h.shape
    rank = w_a1.shape[-1]
    wa1 = jnp.concatenate([w_a1[0], w_a1[1]], axis=1).astype(BF16)
    zeros = jnp.zeros((rank, GLA_DK), F32)
    wa2 = jnp.concatenate([jnp.concatenate([w_a2[0], zeros], axis=1),
                           jnp.concatenate([zeros, w_a2[1]], axis=1)], axis=0).astype(BF16)
    ba = jnp.concatenate([b_a[0], b_a[1]]).reshape(1, 2 * GLA_DK)
    h2 = h.reshape(b * s, d)
    q, k, v, r, la = _gla_proj(h2, g, w_in.astype(BF16), wa1, wa2, ba)
    o_f, o_b = _gla_scan(q, k, v, la, b, s)
    return _gla_out_ffn(o_f, o_b, r, norm_g.reshape(1, GLA_DV), h2, w_o.astype(BF16), ffn_params,
                        final=final).reshape(b, s, d)


def kernel(x, norm_g, final_norm_g, ffn_w1, ffn_w3, ffn_w2, attn_w_qkv, attn_w_o, conv_w_pw1, conv_b_pw1,
           conv_w_dw, conv_b_dw, conv_ln_g, conv_ln_b, conv_w_pw2, conv_b_pw2, fnet_w, fnet_b, gla_w_in,
           gla_w_a1, gla_w_a2, gla_b_a, gla_norm_g, gla_w_o):
    b, s, d = x.shape
    depth = norm_g.shape[0]
    n_mixers = 4
    gf = final_norm_g.reshape(1, d)
    h = x

    def ffn_params(i, half):
        return (norm_g[i, 2 * half].reshape(1, d), ffn_w1[i, half].astype(BF16), ffn_w3[i, half].astype(BF16),
                ffn_w2[i, half].astype(BF16), gf)

    def ffn(h, i, half, final=False):
        return _ffn(h.reshape(b * s, d), *ffn_params(i, half), final=final).reshape(b, s, d)

    for i in range(depth):
        m, j = i % n_mixers, i // n_mixers
        final = i == depth - 1
        h = ffn(h, i, 0)
        g = norm_g[i, 1].reshape(1, d)
        post = ffn_params(i, 1)
        if m == 0:
            h = _attention_mixer_ffn(h, g, attn_w_qkv[j], attn_w_o[j], post, final=final)
        elif m == 1:
            h = _conv_mixer(h, g, conv_w_pw1[j], conv_b_pw1[j], conv_w_dw[j], conv_b_dw[j], conv_ln_g[j],
                            conv_ln_b[j], conv_w_pw2[j], conv_b_pw2[j])
            h = ffn(h, i, 1, final=final)
        elif m == 2:
            h = _fourier_mixer(h, g, fnet_w[j], fnet_b[j])
            h = ffn(h, i, 1, final=final)
        else:
            h = _gla_mixer_ffn(h, g, gla_w_in[j], gla_w_a1[j], gla_w_a2[j], gla_b_a[j], gla_norm_g[j],
                               gla_w_o[j], post, final=final)
    return h
```
